```python
import math
import jax
import jax.numpy as jnp
from jax import lax
import numpy as np

D_MODEL = 1024
BATCH = 4
SEQ = 4096
DEPTH = 2
DEC_BATCH = 32
DEC_SEQ = 4
PAST_LEN = 8192
PAGE_SIZE = 128

HEAD_DIM = 64
W_ATT = D_MODEL // 2
N_HEADS = W_ATT // HEAD_DIM
W_POOL = D_MODEL // 4
N_POOL_GROUPS = 4
POOL_GW = W_POOL // N_POOL_GROUPS
POOL_WINDOWS = (2, 4, 8, 16)
POOL_BUF = max(POOL_WINDOWS) - 1
W_CONV = D_MODEL // 4
CONV_W = 31
CONV_BUF = CONV_W - 1
N_BRANCH = 3
D_FF = ((8 * D_MODEL // 3 + 255) // 256) * 256
Q_BLOCK = 128
LN_EPS = 1e-5
ALPHA = (2.0 * DEPTH) ** 0.25
BETA = (8.0 * DEPTH) ** -0.25
SB_BIAS_INIT = -7.0
N_IN = W_POOL + 3 * W_ATT + 2 * W_CONV + N_BRANCH * D_MODEL

kernel_name = 'hybrid_pool_stickbreak_conformer_decoder'


def _layernorm(x, g, b):
    xf = x.astype(jnp.float32)
    mu = xf.mean(-1, keepdims=True)
    var = jnp.square(xf - mu).mean(-1, keepdims=True)
    y = (xf - mu) * lax.rsqrt(var + LN_EPS) * g.astype(jnp.float32) + b.astype(jnp.float32)
    return y.astype(x.dtype)


def _pool_branch(u, buf, start_pos):
    B, T, C = u.shape
    ext = jnp.concatenate([buf, u], axis=1)
    csum = jnp.cumsum(ext.astype(jnp.float32), axis=1)
    csum = jnp.concatenate([jnp.zeros((B, 1, C), jnp.float32), csum], axis=1)
    c_end = csum[:, POOL_BUF + 1:]
    pos = start_pos + jnp.arange(T)
    means = []
    for g, w in enumerate(POOL_WINDOWS):
        lo, hi = g * POOL_GW, (g + 1) * POOL_GW
        st = POOL_BUF + 1 - w
        win_sum = c_end[..., lo:hi] - csum[:, st:st + T, lo:hi]
        count = jnp.minimum(pos + 1, w).astype(jnp.float32)[None, :, None]
        means.append(win_sum / count)
    mean = jnp.stack(means, axis=2)
    p = mean - u.reshape(B, T, N_POOL_GROUPS, POOL_GW).astype(jnp.float32)
    return p.astype(u.dtype), ext[:, -POOL_BUF:]


def _dwconv(g, buf, w, b):
    ext = jnp.concatenate([buf, g], axis=1)
    out = lax.conv_general_dilated(ext, w[:, None, :].astype(ext.dtype), (1,), 'VALID',
                                   dimension_numbers=('NWC', 'WIO', 'NWC'),
                                   feature_group_count=g.shape[-1])
    return out + b.astype(out.dtype), ext[:, -CONV_BUF:]


def _sb_attend(q, k, v, q_pos, k_pos, sb_bias):
    z = jnp.einsum('bqhd,bkhd->bhqk', q, k, preferred_element_type=jnp.float32) * (HEAD_DIM ** -0.5)
    z = z + sb_bias.astype(jnp.float32)[None, :, None, None]
    mask = k_pos[None, :] < q_pos[:, None]
    log_rest = jnp.where(mask, jax.nn.log_sigmoid(-z), 0.0)
    log_pass = lax.cumsum(log_rest, axis=3, reverse=True) - log_rest
    a = jnp.where(mask, jnp.exp(jax.nn.log_sigmoid(z) + log_pass), 0.0)
    return jnp.einsum('bhqk,bkhd->bqhd', a.astype(v.dtype), v)


def _sb_prompt(q, k, v, sb_bias):
    B, S, H, Dh = q.shape
    k_pos = jnp.arange(S)

    def blk(i):
        qb = lax.dynamic_slice_in_dim(q, i * Q_BLOCK, Q_BLOCK, axis=1)
        return _sb_attend(qb, k, v, i * Q_BLOCK + jnp.arange(Q_BLOCK), k_pos, sb_bias)

    o = lax.map(blk, jnp.arange(S // Q_BLOCK))
    return o.transpose(1, 0, 2, 3, 4).reshape(B, S, H, Dh)


def _token_mixer(x, kv_past, pool_buf, conv_buf, start_pos, w_in, b_gate, sb_bias, pool_w, pool_scale,
                 w_att_o, conv_dw, conv_b, conv_ln_g, conv_ln_b, conv_pw, w_out):
    B, T, _ = x.shape
    h = x @ w_in
    o1 = W_POOL
    o2 = o1 + 3 * W_ATT
    o3 = o2 + 2 * W_CONV
    u_a = h[..., :o1]
    q, k, v = jnp.split(h[..., o1:o2], 3, axis=-1)
    q = q.reshape(B, T, N_HEADS, HEAD_DIM)
    k = k.reshape(B, T, N_HEADS, HEAD_DIM)
    v = v.reshape(B, T, N_HEADS, HEAD_DIM)
    c_in = h[..., o2:o3]
    gates = jax.nn.sigmoid(h[..., o3:] + b_gate).reshape(B, T, N_BRANCH, D_MODEL)
    p, new_pool = _pool_branch(u_a, pool_buf, start_pos)
    br_a = jnp.einsum('btgc,gcd->btgd', p, pool_w).reshape(B, T, D_MODEL) * pool_scale
    if kv_past is None:
        o = _sb_prompt(q, k, v, sb_bias)
    else:
        k_past, v_past = kv_past
        P = k_past.shape[1]
        o = _sb_attend(q, jnp.concatenate([k_past, k], axis=1), jnp.concatenate([v_past, v], axis=1),
                       start_pos + jnp.arange(T), jnp.arange(P + T), sb_bias)
    br_b = o.reshape(B, T, W_ATT) @ w_att_o
    glu = c_in[..., :W_CONV] * jax.nn.sigmoid(c_in[..., W_CONV:])
    cc, new_conv = _dwconv(glu, conv_buf, conv_dw, conv_b)
    br_c = jax.nn.silu(_layernorm(cc, conv_ln_g, conv_ln_b)) @ conv_pw
    merged = gates[:, :, 0] * br_a + gates[:, :, 1] * br_b + gates[:, :, 2] * br_c
    return merged @ w_out, (k, v, new_pool, new_conv)


def _ffn(x, wg, wu, wd):
    return (jax.nn.silu(x @ wg) * (x @ wu)) @ wd


def setup_inputs(seed: int = 0) -> dict:
    key = jax.random.key(seed)
    ks = jax.random.split(key, 32)
    n_pages = PAST_LEN // PAGE_SIZE
    used = DEC_BATCH * n_pages
    n_pool_pages = used + max(1, used // 4)
    f32 = jnp.float32

    def nrm(k, shape, scale):
        return jax.random.normal(k, shape, f32) * scale

    page_table = jax.random.permutation(ks[0], n_pool_pages)[:used].reshape(DEC_BATCH, n_pages).astype(jnp.int32)
    return {
        'x_prompt': nrm(ks[1], (BATCH, SEQ, D_MODEL), 1.0),
        'x_sample': nrm(ks[2], (DEC_BATCH, DEC_SEQ, D_MODEL), 1.0),
        'cache_k': nrm(ks[3], (DEPTH, n_pool_pages, PAGE_SIZE, N_HEADS, HEAD_DIM), 1.0),
        'cache_v': nrm(ks[4], (DEPTH, n_pool_pages, PAGE_SIZE, N_HEADS, HEAD_DIM), 1.0),
        'state_pool': nrm(ks[5], (DEPTH, DEC_BATCH, POOL_BUF, W_POOL), 1.0),
        'state_conv': nrm(ks[6], (DEPTH, DEC_BATCH, CONV_BUF, W_CONV), 0.5),
        'page_table': page_table,
        'w_in': nrm(ks[7], (DEPTH, D_MODEL, N_IN), D_MODEL ** -0.5),
        'b_gate': nrm(ks[8], (DEPTH, N_BRANCH * D_MODEL), 0.02),
        'sb_bias': SB_BIAS_INIT + nrm(ks[25], (DEPTH, N_HEADS), 0.5),
        'pool_w': nrm(ks[9], (DEPTH, N_POOL_GROUPS, POOL_GW, D_MODEL // N_POOL_GROUPS), POOL_GW ** -0.5),
        'pool_scale': 1.0 + nrm(ks[10], (DEPTH, D_MODEL), 0.1),
        'w_att_o': nrm(ks[11], (DEPTH, W_ATT, D_MODEL), W_ATT ** -0.5),
        'conv_dw': nrm(ks[12], (DEPTH, CONV_W, W_CONV), CONV_W ** -0.5),
        'conv_b': nrm(ks[13], (DEPTH, W_CONV), 0.02),
        'conv_ln_g': 1.0 + nrm(ks[14], (DEPTH, W_CONV), 0.05),
        'conv_ln_b': nrm(ks[15], (DEPTH, W_CONV), 0.02),
        'conv_pw': nrm(ks[16], (DEPTH, W_CONV, D_MODEL), W_CONV ** -0.5),
        'w_out': nrm(ks[17], (DEPTH, D_MODEL, D_MODEL), BETA * D_MODEL ** -0.5),
        'ln1_g': 1.0 + nrm(ks[18], (DEPTH, D_MODEL), 0.05),
        'ln1_b': nrm(ks[19], (DEPTH, D_MODEL), 0.02),
        'ffn_w_gate': nrm(ks[20], (DEPTH, D_MODEL, D_FF), D_MODEL ** -0.5),
        'ffn_w_up': nrm(ks[21], (DEPTH, D_MODEL, D_FF), D_MODEL ** -0.5),
        'ffn_w_down': nrm(ks[22], (DEPTH, D_FF, D_MODEL), BETA * D_FF ** -0.5),
        'ln2_g': 1.0 + nrm(ks[23], (DEPTH, D_MODEL), 0.05),
        'ln2_b': nrm(ks[24], (DEPTH, D_MODEL), 0.02),
    }


def reference(x_prompt, x_sample, cache_k, cache_v, state_pool, state_conv, page_table,
              w_in, b_gate, sb_bias, pool_w, pool_scale, w_att_o, conv_dw, conv_b, conv_ln_g, conv_ln_b,
              conv_pw, w_out, ln1_g, ln1_b, ffn_w_gate, ffn_w_up, ffn_w_down, ln2_g, ln2_b):
    B = x_prompt.shape[0]
    DB, n_pages = page_table.shape
    past = n_pages * cache_k.shape[2]
    xp, xs = x_prompt, x_sample
    kp_l, vp_l, pp_l, cp_l = [], [], [], []
    ks_l, vs_l, ps_l, cs_l = [], [], [], []
    for l in range(DEPTH):
        mix_w = (w_in[l], b_gate[l], sb_bias[l], pool_w[l], pool_scale[l], w_att_o[l], conv_dw[l], conv_b[l],
                 conv_ln_g[l], conv_ln_b[l], conv_pw[l], w_out[l])
        pool0 = jnp.zeros((B, POOL_BUF, W_POOL), xp.dtype)
        conv0 = jnp.zeros((B, CONV_BUF, W_CONV), xp.dtype)
        mp, (k_new, v_new, pool_new, conv_new) = _token_mixer(xp, None, pool0, conv0, 0, *mix_w)
        xp = _layernorm(ALPHA * xp + mp, ln1_g[l], ln1_b[l])
        xp = _layernorm(ALPHA * xp + _ffn(xp, ffn_w_gate[l], ffn_w_up[l], ffn_w_down[l]), ln2_g[l], ln2_b[l])
        kp_l.append(k_new); vp_l.append(v_new); pp_l.append(pool_new); cp_l.append(conv_new)
        k_past = cache_k[l][page_table].reshape(DB, past, N_HEADS, HEAD_DIM)
        v_past = cache_v[l][page_table].reshape(DB, past, N_HEADS, HEAD_DIM)
        ms, (k_new, v_new, pool_new, conv_new) = _token_mixer(xs, (k_past, v_past), state_pool[l],
                                                            state_conv[l], past, *mix_w)
        xs = _layernorm(ALPHA * xs + ms, ln1_g[l], ln1_b[l])
        xs = _layernorm(ALPHA * xs + _ffn(xs, ffn_w_gate[l], ffn_w_up[l], ffn_w_down[l]), ln2_g[l], ln2_b[l])
        ks_l.append(k_new); vs_l.append(v_new); ps_l.append(pool_new); cs_l.append(conv_new)
    return (xp, xs, jnp.stack(kp_l), jnp.stack(vp_l), jnp.stack(pp_l), jnp.stack(cp_l),
            jnp.stack(ks_l), jnp.stack(vs_l), jnp.stack(ps_l), jnp.stack(cs_l))
```

```python
import functools

import jax
import jax.numpy as jnp
from jax import lax
from jax.experimental import pallas as pl
from jax.experimental.pallas import tpu as pltpu

HEAD_DIM = 64
HEAD_PAIR = 2 * HEAD_DIM
POOL_WINDOWS = (2, 4, 8, 16)
POOL_BUF = max(POOL_WINDOWS) - 1
POOL_HIST = 16
CONV_W = 31
CONV_BUF = CONV_W - 1
CONV_HIST = 32
N_BRANCH = 3
LN_EPS = 1e-5
LANES = 128
VMEM_LIMIT = 56 * 1024 * 1024

F32 = jnp.float32
BF16 = jnp.bfloat16


def _cparams(sem):
    return pltpu.CompilerParams(dimension_semantics=sem, vmem_limit_bytes=VMEM_LIMIT)


def _const_spec(shape):
    nd = len(shape)
    return pl.BlockSpec(shape, lambda *_: (0,) * nd, pipeline_mode=pl.Buffered(1))


def _layernorm(x, g, b):
    mu = jnp.mean(x, axis=-1, keepdims=True)
    xc = x - mu
    var = jnp.mean(xc * xc, axis=-1, keepdims=True)
    return xc * lax.rsqrt(var + LN_EPS) * g + b


def _dot(a, b):
    return jnp.dot(a, b, preferred_element_type=F32)


def _proj_kernel(x_ref, w_ref, bg_ref, ua_ref, q_ref, k_ref, v_ref, kb_ref, vb_ref, glu_ref, gate_ref,
                 *, d_pool, d_att, d_conv, d_gate, chunk):
    x = x_ref[...].astype(BF16)

    def seg(lo, n):
        return _dot(x, w_ref[:, lo:lo + n])

    o = 0
    ua_ref[...] = seg(o, d_pool)
    o += d_pool
    q_ref[...] = (seg(o, d_att) * (HEAD_DIM ** -0.5)).astype(BF16)
    o += d_att
    kk = seg(o, d_att)
    k_ref[...] = kk
    kb_ref[...] = kk.astype(BF16)
    o += d_att
    vv = seg(o, d_att)
    v_ref[...] = vv
    vb_ref[...] = vv.astype(BF16)
    o += d_att
    glu_ref[...] = seg(o, d_conv) * jax.nn.sigmoid(seg(o + d_conv, d_conv))
    o += 2 * d_conv
    for c in range(0, d_gate, chunk):
        n = min(chunk, d_gate - c)
        gate_ref[:, c:c + n] = jax.nn.sigmoid(seg(o + c, n) + bg_ref[:, c:c + n]).astype(BF16)


def _proj(x, w_in, b_gate, tm):
    m, d = x.shape
    n_in = w_in.shape[1]
    d_pool = d // 4
    d_att = d // 2
    d_conv = d // 4
    d_gate = N_BRANCH * d
    assert n_in == d_pool + 3 * d_att + 2 * d_conv + d_gate and m % tm == 0
    row = lambda n: pl.BlockSpec((tm, n), lambda i: (i, 0))
    kern = functools.partial(_proj_kernel, d_pool=d_pool, d_att=d_att, d_conv=d_conv, d_gate=d_gate, chunk=512)
    return pl.pallas_call(
        kern,
        grid=(m // tm,),
        in_specs=[row(d), _const_spec((d, n_in)), _const_spec((1, d_gate))],
        out_specs=[row(d_pool), row(d_att), row(d_att), row(d_att), row(d_att), row(d_att), row(d_conv), row(d_gate)],
        out_shape=[jax.ShapeDtypeStruct((m, d_pool), F32), jax.ShapeDtypeStruct((m, d_att), BF16),
                   jax.ShapeDtypeStruct((m, d_att), F32), jax.ShapeDtypeStruct((m, d_att), F32),
                   jax.ShapeDtypeStruct((m, d_att), BF16), jax.ShapeDtypeStruct((m, d_att), BF16),
                   jax.ShapeDtypeStruct((m, d_conv), F32), jax.ShapeDtypeStruct((m, d_gate), BF16)],
        compiler_params=_cparams(("parallel",)),
        name="proj",
    )(x, w_in, b_gate.reshape(1, d_gate))


def _pool_windows(load_shifted, ua, pos, c_pool):
    gw = c_pool // len(POOL_WINDOWS)
    col = lax.broadcasted_iota(jnp.int32, ua.shape, ua.ndim - 1)
    run = ua
    win = jnp.zeros_like(ua)
    width = jnp.zeros(ua.shape, jnp.int32)
    for d in range(1, max(POOL_WINDOWS)):
        run = run + load_shifted(d)
        if d + 1 in POOL_WINDOWS:
            g = POOL_WINDOWS.index(d + 1)
            in_group = (col >= g * gw) & (col < (g + 1) * gw)
            win = jnp.where(in_group, run, win)
            width = jnp.where(in_group, d + 1, width)
    count = jnp.minimum(pos + 1, width).astype(F32)
    return win / count - ua


def _conv_act(load_shifted, dw_ref, cb_ref, lg_ref, lb_ref):
    acc = None
    for i in range(CONV_W):
        term = load_shifted(CONV_BUF - i) * dw_ref[i]
        acc = term if acc is None else acc + term
    cc = _layernorm(acc + cb_ref[...], lg_ref[...], lb_ref[...])
    return cc * jax.nn.sigmoid(cc)


def _branch_prompt_kernel(ua_ref, uah_ref, glu_ref, gluh_ref, dw_ref, cb_ref, lg_ref, lb_ref,
                          p_ref, act_ref, pool_ext, conv_ext, *, ts):
    i = pl.program_id(1)
    c_pool = ua_ref.shape[-1]
    live = (i > 0).astype(F32)
    pool_ext[0:POOL_HIST, :] = uah_ref[...] * live
    pool_ext[POOL_HIST:, :] = ua_ref[...]
    conv_ext[0:CONV_HIST, :] = gluh_ref[...] * live
    conv_ext[CONV_HIST:, :] = glu_ref[...]
    pos = i * ts + lax.broadcasted_iota(jnp.int32, (ts, c_pool), 0)
    p = _pool_windows(lambda d: pool_ext[pl.ds(POOL_HIST - d, ts), :], ua_ref[...], pos, c_pool)
    p_ref[...] = p.astype(BF16)
    act = _conv_act(lambda d: conv_ext[pl.ds(CONV_HIST - d, ts), :], dw_ref, cb_ref, lg_ref, lb_ref)
    act_ref[...] = act.astype(BF16)


def _branch_prompt(ua, glu, conv_dw, conv_b, ln_g, ln_b, batch, seq, ts):
    m, c = ua.shape
    assert seq % ts == 0 and ts % CONV_HIST == 0
    nt = seq // ts
    tile = pl.BlockSpec((ts, c), lambda b, i: (b * nt + i, 0))

    def hist(h):
        per = ts // h
        return pl.BlockSpec((h, c), lambda b, i: (jnp.maximum((b * nt + i) * per - 1, 0), 0))

    vec = _const_spec((1, c))
    return pl.pallas_call(
        functools.partial(_branch_prompt_kernel, ts=ts),
        grid=(batch, nt),
        in_specs=[tile, hist(POOL_HIST), tile, hist(CONV_HIST), _const_spec((CONV_W, 1, c)), vec, vec, vec],
        out_specs=[tile, tile],
        out_shape=[jax.ShapeDtypeStruct((m, c), BF16), jax.ShapeDtypeStruct((m, c), BF16)],
        scratch_shapes=[pltpu.VMEM((POOL_HIST + ts, c), F32), pltpu.VMEM((CONV_HIST + ts, c), F32)],
        compiler_params=_cparams(("parallel", "parallel")),
        name="branch_prompt",
    )(ua, ua, glu, glu, conv_dw.reshape(CONV_W, 1, c), conv_b.reshape(1, c), ln_g.reshape(1, c), ln_b.reshape(1, c))


def _branch_sample_kernel(pext_ref, cext_ref, dw_ref, cb_ref, lg_ref, lb_ref, p_ref, act_ref, *, start_pos, t_new):
    c_pool = pext_ref.shape[-1]
    shape = pext_ref.shape[1:]
    for t in range(t_new):
        pos = jnp.full(shape, start_pos + t, jnp.int32)
        p = _pool_windows(lambda d: pext_ref[POOL_BUF + t - d], pext_ref[POOL_BUF + t], pos, c_pool)
        p_ref[t] = p.astype(BF16)
        act = _conv_act(lambda d: cext_ref[CONV_BUF + t - d], dw_ref, cb_ref, lg_ref, lb_ref)
        act_ref[t] = act.astype(BF16)


def _branch_sample(pext, cext, conv_dw, conv_b, ln_g, ln_b, start_pos, t_new):
    _, db, c = pext.shape
    return pl.pallas_call(
        functools.partial(_branch_sample_kernel, start_pos=start_pos, t_new=t_new),
        out_shape=[jax.ShapeDtypeStruct((t_new, db, c), BF16), jax.ShapeDtypeStruct((t_new, db, c), BF16)],
        name="branch_sample",
    )(pext, cext, conv_dw.reshape(CONV_W, 1, c), conv_b.reshape(1, c), ln_g.reshape(1, c), ln_b.reshape(1, c))


def _suffix_matrix(tk):
    j = lax.broadcasted_iota(jnp.int32, (tk, tk + LANES), 0)
    s = lax.broadcasted_iota(jnp.int32, (tk, tk + LANES), 1)
    return jnp.where((j > s) | (s >= tk), 1.0, 0.0).astype(BF16)


def _sb_block(z, mask, u, carry):
    tk = z.shape[1]
    log_rest = -(jnp.maximum(z, 0.0) + jnp.log1p(jnp.exp(-jnp.abs(z))))
    if mask is not None:
        log_rest = jnp.where(mask, log_rest, 0.0)
    sums = _dot(log_rest.astype(BF16), u)
    right = jnp.concatenate([carry] * (tk // LANES), axis=1)
    a = jnp.exp(z + log_rest + sums[:, :tk] + right)
    if mask is not None:
        a = jnp.where(mask, a, 0.0)
    return a, carry + sums[:, tk:]


def _attn_prompt_kernel(bias_ref, q_ref, k_ref, v_ref, u_ref, o_ref, acc_ref, carry_ref, *, tq):
    hp = pl.program_id(1)
    qi = pl.program_id(2)
    lane = lax.broadcasted_iota(jnp.int32, (tq, HEAD_PAIR), 1)
    first = lane < HEAD_DIM
    q = q_ref[...]
    zero = jnp.zeros_like(q)
    q_heads = (jnp.where(first, q, zero), jnp.where(first, zero, q))
    bias = (bias_ref[2 * hp], bias_ref[2 * hp + 1])
    acc_ref[...] = jnp.zeros_like(acc_ref)
    carry_ref[...] = jnp.zeros_like(carry_ref)
    row = lax.broadcasted_iota(jnp.int32, (tq, tq), 0)
    colk = lax.broadcasted_iota(jnp.int32, (tq, tq), 1)
    causal = colk < row

    def block(j, mask):
        start = pl.multiple_of(j * tq, tq)
        kb = k_ref[pl.ds(start, tq), :]
        vb = v_ref[pl.ds(start, tq), :]
        for h in range(2):
            z = lax.dot_general(q_heads[h], kb, (((1,), (1,)), ((), ())), preferred_element_type=F32) + bias[h]
            a, carry = _sb_block(z, mask, u_ref[...], carry_ref[h])
            carry_ref[h] = carry
            acc_ref[h] += _dot(a.astype(BF16), vb)

    block(qi, causal)

    def body(jj, c):
        block(qi - 1 - jj, None)
        return c

    lax.fori_loop(0, qi, body, 0)
    o_ref[...] = jnp.where(first, acc_ref[0], acc_ref[1]).astype(BF16)


def _attn_prompt(q, kb, vb, sb_bias, batch, seq, tq):
    m, w = q.shape
    assert seq % tq == 0 and w % HEAD_PAIR == 0
    nq = seq // tq
    qspec = pl.BlockSpec((tq, HEAD_PAIR), lambda b, h, i: (b * nq + i, h))
    kvspec = pl.BlockSpec((seq, HEAD_PAIR), lambda b, h, i: (b, h))
    return pl.pallas_call(
        functools.partial(_attn_prompt_kernel, tq=tq),
        grid=(batch, w // HEAD_PAIR, nq),
        in_specs=[pl.BlockSpec(memory_space=pltpu.SMEM), qspec, kvspec, kvspec, _const_spec((tq, tq + LANES))],
        out_specs=qspec,
        out_shape=jax.ShapeDtypeStruct((m, w), BF16),
        scratch_shapes=[pltpu.VMEM((2, tq, HEAD_PAIR), F32), pltpu.VMEM((2, tq, LANES), F32)],
        compiler_params=_cparams(("parallel", "parallel", "arbitrary")),
        name="attn_prompt",
    )(sb_bias, q, kb, vb, _suffix_matrix(tq))


def _attn_sample_kernel(pt_ref, bias_ref, q_ref, kn_ref, vn_ref, u_ref, *rest, n_heads, t_new, pages_per_step):
    k_refs = rest[:pages_per_step]
    v_refs = rest[pages_per_step:2 * pages_per_step]
    o_ref, qrow_ref, acc_ref, carry_ref = rest[2 * pages_per_step:]
    step = pl.program_id(1)
    rows = t_new * n_heads
    width = n_heads * HEAD_DIM
    page = kn_ref.shape[0]
    head_of_col = lax.broadcasted_iota(jnp.int32, (rows, width), 1) // HEAD_DIM
    head_of_row = lax.broadcasted_iota(jnp.int32, (rows, width), 0) % n_heads
    own = head_of_col == head_of_row

    def block(kb, vb, mask):
        z = lax.dot_general(qrow_ref[...], kb, (((1,), (1,)), ((), ())), preferred_element_type=F32)
        z = z + bias_ref[...]
        a, carry = _sb_block(z, mask, u_ref[...], carry_ref[...])
        carry_ref[...] = carry
        acc_ref[...] += _dot(a.astype(BF16), vb)

    @pl.when(step == 0)
    def _():
        q = q_ref[...]
        qrep = jnp.broadcast_to(q[:, None, :], (t_new, n_heads, width)).reshape(rows, width)
        qrow_ref[...] = jnp.where(own, qrep, jnp.zeros_like(qrep))
        acc_ref[...] = jnp.zeros_like(acc_ref)
        carry_ref[...] = jnp.zeros_like(carry_ref)
        t_of_row = lax.broadcasted_iota(jnp.int32, (rows, page), 0) // n_heads
        key = lax.broadcasted_iota(jnp.int32, (rows, page), 1)
        block(kn_ref[...], vn_ref[...], key < t_of_row)

    for i in range(pages_per_step):
        block(k_refs[i][...].astype(BF16), v_refs[i][...].astype(BF16), None)

    @pl.when(step == pl.num_programs(1) - 1)
    def _():
        picked = jnp.where(own, acc_ref[...], 0.0).reshape(t_new, n_heads, width)
        o_ref[...] = jnp.sum(picked, axis=1).astype(BF16)


def _attn_sample(q, k_new, v_new, cache_k, cache_v, page_base, page_table, sb_bias, pages_per_step):
    db, t_new, w = q.shape
    n_heads = w // HEAD_DIM
    n_pages = page_table.shape[1]
    page = cache_k.shape[1]
    assert n_pages % pages_per_step == 0 and page % LANES == 0 and t_new <= page
    rows = t_new * n_heads
    pad = ((0, 0), (0, page - t_new), (0, 0))
    kn = jnp.pad(k_new, pad)
    vn = jnp.pad(v_new, pad)
    bias_rows = jnp.tile(sb_bias, t_new).reshape(rows, 1).astype(F32)
    base = page_base

    def page_spec(i):
        def idx(b, s, pt):
            return (base + pt[b * n_pages + (n_pages - 1 - (s * pages_per_step + i))], 0, 0)
        return pl.BlockSpec((None, page, w), idx)

    per_b = lambda n: pl.BlockSpec((None, n, w), lambda b, s, pt: (b, 0, 0))
    const2 = lambda shape: pl.BlockSpec(shape, lambda b, s, pt: (0, 0))
    grid_spec = pltpu.PrefetchScalarGridSpec(
        num_scalar_prefetch=1,
        grid=(db, n_pages // pages_per_step),
        in_specs=[const2((rows, 1)), per_b(t_new), per_b(page), per_b(page), const2((page, page + LANES))]
                 + [page_spec(i) for i in range(pages_per_step)] * 2,
        out_specs=per_b(t_new),
        scratch_shapes=[pltpu.VMEM((rows, w), BF16), pltpu.VMEM((rows, w), F32), pltpu.VMEM((rows, LANES), F32)],
    )
    return pl.pallas_call(
        functools.partial(_attn_sample_kernel, n_heads=n_heads, t_new=t_new, pages_per_step=pages_per_step),
        grid_spec=grid_spec,
        out_shape=jax.ShapeDtypeStruct((db, t_new, w), BF16),
        compiler_params=_cparams(("parallel", "arbitrary")),
        name="attn_sample",
    )(page_table.reshape(-1), bias_rows, q, kn, vn, _suffix_matrix(page),
      *([cache_k] * pages_per_step), *([cache_v] * pages_per_step))


def _merge_kernel(x_ref, p_ref, o_ref, act_ref, gate_ref, wp_ref, ps_ref, wo_ref, wc_ref, wout_ref, lg_ref, lb_ref,
                  y_ref, *, alpha):
    d = x_ref.shape[-1]
    br_a = _dot(p_ref[...], wp_ref[...]) * ps_ref[...]
    br_b = _dot(o_ref[...], wo_ref[...])
    br_c = _dot(act_ref[...], wc_ref[...])
    merged = (gate_ref[:, 0:d].astype(F32) * br_a + gate_ref[:, d:2 * d].astype(F32) * br_b
              + gate_ref[:, 2 * d:3 * d].astype(F32) * br_c)
    y = _dot(merged.astype(BF16), wout_ref[...])
    y_ref[...] = _layernorm(alpha * x_ref[...] + y, lg_ref[...], lb_ref[...])


def _merge(x, p, o, act, gates, w_pool, pool_scale, w_att_o, conv_pw, w_out, ln_g, ln_b, alpha, tm):
    m, d = x.shape
    assert m % tm == 0
    row = lambda a: pl.BlockSpec((tm, a.shape[1]), lambda i: (i, 0))
    full = lambda a: _const_spec(a.shape)
    vec = lambda a: a.reshape(1, d)
    args = (x, p, o, act, gates, w_pool, vec(pool_scale), w_att_o, conv_pw, w_out, vec(ln_g), vec(ln_b))
    return pl.pallas_call(
        functools.partial(_merge_kernel, alpha=alpha),
        grid=(m // tm,),
        in_specs=[row(a) for a in args[:5]] + [full(a) for a in args[5:]],
        out_specs=pl.BlockSpec((tm, d), lambda i: (i, 0)),
        out_shape=jax.ShapeDtypeStruct((m, d), F32),
        compiler_params=_cparams(("parallel",)),
        name="merge",
    )(*args)


def _ffn_kernel(x_ref, wg_ref, wu_ref, wd_ref, lg_ref, lb_ref, y_ref, *, alpha, chunk):
    x = x_ref[...]
    xb = x.astype(BF16)
    d_ff = wg_ref.shape[1]
    acc = None
    for c in range(0, d_ff, chunk):
        n = min(chunk, d_ff - c)
        g = _dot(xb, wg_ref[:, c:c + n])
        u = _dot(xb, wu_ref[:, c:c + n])
        h = (g * jax.nn.sigmoid(g) * u).astype(BF16)
        part = _dot(h, wd_ref[c:c + n, :])
        acc = part if acc is None else acc + part
    y_ref[...] = _layernorm(alpha * x + acc, lg_ref[...], lb_ref[...])


def _ffn(x, wg, wu, wd, ln_g, ln_b, alpha, tm):
    m, d = x.shape
    assert m % tm == 0
    vec = lambda a: a.reshape(1, d)
    return pl.pallas_call(
        functools.partial(_ffn_kernel, alpha=alpha, chunk=512),
        grid=(m // tm,),
        in_specs=[pl.BlockSpec((tm, d), lambda i: (i, 0)), _const_spec(wg.shape), _const_spec(wu.shape),
                  _const_spec(wd.shape), _const_spec((1, d)), _const_spec((1, d))],
        out_specs=pl.BlockSpec((tm, d), lambda i: (i, 0)),
        out_shape=jax.ShapeDtypeStruct((m, d), F32),
        compiler_params=_cparams(("parallel",)),
        name="ffn",
    )(x, wg, wu, wd, vec(ln_g), vec(ln_b))


def _row_tile(m, want):
    t = min(want, m)
    while m % t:
        t //= 2
    return t


def kernel(x_prompt, x_sample, cache_k, cache_v, state_pool, state_conv, page_table, w_in, b_gate, sb_bias, pool_w, pool_scale, w_att_o, conv_dw, conv_b, conv_ln_g, conv_ln_b, conv_pw, w_out, ln1_g, ln1_b, ffn_w_gate, ffn_w_up, ffn_w_down, ln2_g, ln2_b):
    depth = w_in.shape[0]
    batch, seq, d = x_prompt.shape
    db, t_new, _ = x_sample.shape
    n_pages = page_table.shape[1]
    page = cache_k.shape[2]
    past = n_pages * page
    alpha = (2.0 * depth) ** 0.25
    w_att = d // 2
    n_heads = w_att // HEAD_DIM
    c_pool = d // 4
    n_groups = len(POOL_WINDOWS)
    gw = c_pool // n_groups

    ck = cache_k.reshape(depth * cache_k.shape[1], page, w_att)
    cv = cache_v.reshape(depth * cache_v.shape[1], page, w_att)

    xp = x_prompt.reshape(batch * seq, d)
    xs = x_sample.reshape(db * t_new, d)
    mp, ms = xp.shape[0], xs.shape[0]
    tm_p = _row_tile(mp, 512)
    tm_s = _row_tile(ms, 128)
    ts = _row_tile(seq, 512)
    tq = _row_tile(seq, 256)

    outs = {n: [] for n in ("kp", "vp", "pp", "cp", "ks", "vs", "ps", "cs")}
    for l in range(depth):
        w_in_b = w_in[l].astype(BF16)
        w_pool = jnp.zeros((c_pool, d), F32)
        for g in range(n_groups):
            w_pool = w_pool.at[g * gw:(g + 1) * gw, g * (d // n_groups):(g + 1) * (d // n_groups)].set(pool_w[l, g])
        w_pool = w_pool.astype(BF16)
        w_o_b = w_att_o[l].astype(BF16)
        w_c_b = conv_pw[l].astype(BF16)
        w_out_b = w_out[l].astype(BF16)
        wg_b = ffn_w_gate[l].astype(BF16)
        wu_b = ffn_w_up[l].astype(BF16)
        wd_b = ffn_w_down[l].astype(BF16)

        def tail(x, p, o, act, gates, tm):
            x1 = _merge(x, p, o, act, gates, w_pool, pool_scale[l], w_o_b, w_c_b, w_out_b, ln1_g[l], ln1_b[l], alpha, tm)
            return _ffn(x1, wg_b, wu_b, wd_b, ln2_g[l], ln2_b[l], alpha, tm)

        ua, q, k, v, kb, vb, glu, gates = _proj(xp, w_in_b, b_gate[l], tm_p)
        p, act = _branch_prompt(ua, glu, conv_dw[l], conv_b[l], conv_ln_g[l], conv_ln_b[l], batch, seq, ts)
        o = _attn_prompt(q, kb, vb, sb_bias[l], batch, seq, tq)
        xp = tail(xp, p, o, act, gates, tm_p)
        outs["kp"].append(k.reshape(batch, seq, n_heads, HEAD_DIM))
        outs["vp"].append(v.reshape(batch, seq, n_heads, HEAD_DIM))
        outs["pp"].append(ua.reshape(batch, seq, c_pool)[:, seq - POOL_BUF:])
        outs["cp"].append(glu.reshape(batch, seq, c_pool)[:, seq - CONV_BUF:])

        ua, q, k, v, kb, vb, glu, gates = _proj(xs, w_in_b, b_gate[l], tm_s)
        pext = jnp.concatenate([state_pool[l], ua.reshape(db, t_new, c_pool)], axis=1)
        cext = jnp.concatenate([state_conv[l], glu.reshape(db, t_new, c_pool)], axis=1)
        p, act = _branch_sample(pext.transpose(1, 0, 2), cext.transpose(1, 0, 2), conv_dw[l], conv_b[l],
                                conv_ln_g[l], conv_ln_b[l], past, t_new)
        p = p.transpose(1, 0, 2).reshape(ms, c_pool)
        act = act.transpose(1, 0, 2).reshape(ms, c_pool)
        o = _attn_sample(q.reshape(db, t_new, w_att), kb.reshape(db, t_new, w_att), vb.reshape(db, t_new, w_att),
                         ck, cv, l * cache_k.shape[1], page_table, sb_bias[l],
                         pages_per_step=_row_tile(n_pages, 8))
        xs = tail(xs, p, o.reshape(ms, w_att), act, gates, tm_s)
        outs["ks"].append(k.reshape(db, t_new, n_heads, HEAD_DIM))
        outs["vs"].append(v.reshape(db, t_new, n_heads, HEAD_DIM))
        outs["ps"].append(pext[:, -POOL_BUF:])
        outs["cs"].append(cext[:, -CONV_BUF:])

    st = lambda n: jnp.stack(outs[n])
    return (xp.reshape(batch, seq, d), xs.reshape(db, t_new, d), st("kp"), st("vp"), st("pp"), st("cp"),
            st("ks"), st("vs"), st("ps"), st("cs"))
```

```python
import functools

import jax
import jax.numpy as jnp
from jax import lax
from jax.experimental import pallas as pl
from jax.experimental.pallas import tpu as pltpu

HEAD_DIM = 64
HEAD_PAIR = 2 * HEAD_DIM
POOL_WINDOWS = (2, 4, 8, 16)
POOL_BUF = max(POOL_WINDOWS) - 1
POOL_HIST = 16
CONV_W = 31
CONV_BUF = CONV_W - 1
CONV_HIST = 32
N_BRANCH = 3
LN_EPS = 1e-5
LANES = 128
VMEM_LIMIT = 56 * 1024 * 1024

F32 = jnp.float32
BF16 = jnp.bfloat16


def _cparams(sem):
    return pltpu.CompilerParams(dimension_semantics=sem, vmem_limit_bytes=VMEM_LIMIT)


def _const_spec(shape):
    nd = len(shape)
    return pl.BlockSpec(shape, lambda *_: (0,) * nd, pipeline_mode=pl.Buffered(1))


def _layernorm(x, g, b):
    mu = jnp.mean(x, axis=-1, keepdims=True)
    xc = x - mu
    var = jnp.mean(xc * xc, axis=-1, keepdims=True)
    return xc * lax.rsqrt(var + LN_EPS) * g + b


def _dot(a, b):
    return jnp.dot(a, b, preferred_element_type=F32)


def _proj_kernel(x_ref, w_ref, bg_ref, ua_ref, q_ref, k_ref, v_ref, kb_ref, vb_ref, glu_ref, gate_ref,
                 *, d_pool, d_att, d_conv, d_gate, chunk):
    x = x_ref[...].astype(BF16)

    def seg(lo, n):
        return _dot(x, w_ref[:, lo:lo + n])

    o = 0
    ua_ref[...] = seg(o, d_pool)
    o += d_pool
    q_ref[...] = (seg(o, d_att) * (HEAD_DIM ** -0.5)).astype(BF16)
    o += d_att
    kk = seg(o, d_att)
    k_ref[...] = kk
    kb_ref[...] = kk.astype(BF16)
    o += d_att
    vv = seg(o, d_att)
    v_ref[...] = vv
    vb_ref[...] = vv.astype(BF16)
    o += d_att
    glu_ref[...] = seg(o, d_conv) * jax.nn.sigmoid(seg(o + d_conv, d_conv))
    o += 2 * d_conv
    for c in range(0, d_gate, chunk):
        n = min(chunk, d_gate - c)
        gate_ref[:, c:c + n] = jax.nn.sigmoid(seg(o + c, n) + bg_ref[:, c:c + n]).astype(BF16)


def _proj(x, w_in, b_gate, tm):
    m, d = x.shape
    n_in = w_in.shape[1]
    d_pool = d // 4
    d_att = d // 2
    d_conv = d // 4
    d_gate = N_BRANCH * d
    assert n_in == d_pool + 3 * d_att + 2 * d_conv + d_gate and m % tm == 0
    row = lambda n: pl.BlockSpec((tm, n), lambda i: (i, 0))
    kern = functools.partial(_proj_kernel, d_pool=d_pool, d_att=d_att, d_conv=d_conv, d_gate=d_gate, chunk=512)
    return pl.pallas_call(
        kern,
        grid=(m // tm,),
        in_specs=[row(d), _const_spec((d, n_in)), _const_spec((1, d_gate))],
        out_specs=[row(d_pool), row(d_att), row(d_att), row(d_att), row(d_att), row(d_att), row(d_conv), row(d_gate)],
        out_shape=[jax.ShapeDtypeStruct((m, d_pool), F32), jax.ShapeDtypeStruct((m, d_att), BF16),
                   jax.ShapeDtypeStruct((m, d_att), F32), jax.ShapeDtypeStruct((m, d_att), F32),
                   jax.ShapeDtypeStruct((m, d_att), BF16), jax.ShapeDtypeStruct((m, d_att), BF16),
                   jax.ShapeDtypeStruct((m, d_conv), F32), jax.ShapeDtypeStruct((m, d_gate), BF16)],
        compiler_params=_cparams(("parallel",)),
        name="proj",
    )(x, w_in, b_gate.reshape(1, d_gate))


def _pool_windows(load_shifted, ua, pos, c_pool):
    gw = c_pool // len(POOL_WINDOWS)
    col = lax.broadcasted_iota(jnp.int32, ua.shape, ua.ndim - 1)
    run = ua
    win = jnp.zeros_like(ua)
    width = jnp.zeros(ua.shape, jnp.int32)
    for d in range(1, max(POOL_WINDOWS)):
        run = run + load_shifted(d)
        if d + 1 in POOL_WINDOWS:
            g = POOL_WINDOWS.index(d + 1)
            in_group = (col >= g * gw) & (col < (g + 1) * gw)
            win = jnp.where(in_group, run, win)
            width = jnp.where(in_group, d + 1, width)
    count = jnp.minimum(pos + 1, width).astype(F32)
    return win / count - ua


def _conv_act(load_shifted, dw_ref, cb_ref, lg_ref, lb_ref):
    acc = None
    for i in range(CONV_W):
        term = load_shifted(CONV_BUF - i) * dw_ref[i]
        acc = term if acc is None else acc + term
    cc = _layernorm(acc + cb_ref[...], lg_ref[...], lb_ref[...])
    return cc * jax.nn.sigmoid(cc)


def _branch_prompt_kernel(ua_ref, uah_ref, glu_ref, gluh_ref, dw_ref, cb_ref, lg_ref, lb_ref,
                          p_ref, act_ref, pool_ext, conv_ext, *, ts):
    i = pl.program_id(1)
    c_pool = ua_ref.shape[-1]
    live = (i > 0).astype(F32)
    pool_ext[0:POOL_HIST, :] = uah_ref[...] * live
    pool_ext[POOL_HIST:, :] = ua_ref[...]
    conv_ext[0:CONV_HIST, :] = gluh_ref[...] * live
    conv_ext[CONV_HIST:, :] = glu_ref[...]
    pos = i * ts + lax.broadcasted_iota(jnp.int32, (ts, c_pool), 0)
    p = _pool_windows(lambda d: pool_ext[pl.ds(POOL_HIST - d, ts), :], ua_ref[...], pos, c_pool)
    p_ref[...] = p.astype(BF16)
    act = _conv_act(lambda d: conv_ext[pl.ds(CONV_HIST - d, ts), :], dw_ref, cb_ref, lg_ref, lb_ref)
    act_ref[...] = act.astype(BF16)


def _branch_prompt(ua, glu, conv_dw, conv_b, ln_g, ln_b, batch, seq, ts):
    m, c = ua.shape
    assert seq % ts == 0 and ts % CONV_HIST == 0
    nt = seq // ts
    tile = pl.BlockSpec((ts, c), lambda b, i: (b * nt + i, 0))

    def hist(h):
        per = ts // h
        return pl.BlockSpec((h, c), lambda b, i: (jnp.maximum((b * nt + i) * per - 1, 0), 0))

    vec = _const_spec((1, c))
    return pl.pallas_call(
        functools.partial(_branch_prompt_kernel, ts=ts),
        grid=(batch, nt),
        in_specs=[tile, hist(POOL_HIST), tile, hist(CONV_HIST), _const_spec((CONV_W, 1, c)), vec, vec, vec],
        out_specs=[tile, tile],
        out_shape=[jax.ShapeDtypeStruct((m, c), BF16), jax.ShapeDtypeStruct((m, c), BF16)],
        scratch_shapes=[pltpu.VMEM((POOL_HIST + ts, c), F32), pltpu.VMEM((CONV_HIST + ts, c), F32)],
        compiler_params=_cparams(("parallel", "parallel")),
        name="branch_prompt",
    )(ua, ua, glu, glu, conv_dw.reshape(CONV_W, 1, c), conv_b.reshape(1, c), ln_g.reshape(1, c), ln_b.reshape(1, c))


def _branch_sample_kernel(pext_ref, cext_ref, dw_ref, cb_ref, lg_ref, lb_ref, p_ref, act_ref, *, start_pos, t_new):
    c_pool = pext_ref.shape[-1]
    shape = pext_ref.shape[1:]
    for t in range(t_new):
        pos = jnp.full(shape, start_pos + t, jnp.int32)
        p = _pool_windows(lambda d: pext_ref[POOL_BUF + t - d], pext_ref[POOL_BUF + t], pos, c_pool)
        p_ref[t] = p.astype(BF16)
        act = _conv_act(lambda d: cext_ref[CONV_BUF + t - d], dw_ref, cb_ref, lg_ref, lb_ref)
        act_ref[t] = act.astype(BF16)


def _branch_sample(pext, cext, conv_dw, conv_b, ln_g, ln_b, start_pos, t_new):
    _, db, c = pext.shape
    return pl.pallas_call(
        functools.partial(_branch_sample_kernel, start_pos=start_pos, t_new=t_new),
        out_shape=[jax.ShapeDtypeStruct((t_new, db, c), BF16), jax.ShapeDtypeStruct((t_new, db, c), BF16)],
        name="branch_sample",
    )(pext, cext, conv_dw.reshape(CONV_W, 1, c), conv_b.reshape(1, c), ln_g.reshape(1, c), ln_b.reshape(1, c))


def _suffix_matrix(tk):
    j = lax.broadcasted_iota(jnp.int32, (tk, tk), 0)
    s = lax.broadcasted_iota(jnp.int32, (tk, tk), 1)
    return jnp.where(j > s, 1.0, 0.0).astype(BF16)


def _sb_block(z, mask, u, carry):
    tk = z.shape[1]
    log_rest = jnp.minimum(-z, 0.0) - jnp.log(1.0 + jnp.exp(-jnp.abs(z)))
    if mask is not None:
        log_rest = jnp.where(mask, log_rest, 0.0)
    sums = _dot(log_rest.astype(BF16), u)
    right = jnp.concatenate([carry] * (tk // LANES), axis=1)
    a = jnp.exp(z + log_rest + sums + right)
    if mask is not None:
        a = jnp.where(mask, a, 0.0)
    return a, carry + jnp.sum(log_rest, axis=1, keepdims=True)


def _attn_prompt_kernel(bias_ref, q_ref, k_ref, v_ref, u_ref, o_ref, acc_ref, carry_ref, *, tq):
    hp = pl.program_id(1)
    qi = pl.program_id(2)
    lane = lax.broadcasted_iota(jnp.int32, (tq, HEAD_PAIR), 1)
    first = lane < HEAD_DIM
    q = q_ref[...]
    zero = jnp.zeros_like(q)
    q_heads = (jnp.where(first, q, zero), jnp.where(first, zero, q))
    bias = (bias_ref[2 * hp], bias_ref[2 * hp + 1])
    acc_ref[...] = jnp.zeros_like(acc_ref)
    carry_ref[...] = jnp.zeros_like(carry_ref)
    row = lax.broadcasted_iota(jnp.int32, (tq, tq), 0)
    colk = lax.broadcasted_iota(jnp.int32, (tq, tq), 1)
    causal = colk < row

    def block(j, mask):
        start = pl.multiple_of(j * tq, tq)
        kb = k_ref[pl.ds(start, tq), :]
        vb = v_ref[pl.ds(start, tq), :]
        for h in range(2):
            z = lax.dot_general(q_heads[h], kb, (((1,), (1,)), ((), ())), preferred_element_type=F32) + bias[h]
            a, carry = _sb_block(z, mask, u_ref[...], carry_ref[h])
            carry_ref[h] = carry
            acc_ref[h] += _dot(a.astype(BF16), vb)

    block(qi, causal)

    def body(jj, c):
        block(qi - 1 - jj, None)
        return c

    lax.fori_loop(0, qi, body, 0)
    o_ref[...] = jnp.where(first, acc_ref[0], acc_ref[1]).astype(BF16)


def _attn_prompt(q, kb, vb, sb_bias, batch, seq, tq):
    m, w = q.shape
    assert seq % tq == 0 and w % HEAD_PAIR == 0
    nq = seq // tq
    qspec = pl.BlockSpec((tq, HEAD_PAIR), lambda b, h, i: (b * nq + i, h))
    kvspec = pl.BlockSpec((seq, HEAD_PAIR), lambda b, h, i: (b, h))
    return pl.pallas_call(
        functools.partial(_attn_prompt_kernel, tq=tq),
        grid=(batch, w // HEAD_PAIR, nq),
        in_specs=[pl.BlockSpec(memory_space=pltpu.SMEM), qspec, kvspec, kvspec, _const_spec((tq, tq))],
        out_specs=qspec,
        out_shape=jax.ShapeDtypeStruct((m, w), BF16),
        scratch_shapes=[pltpu.VMEM((2, tq, HEAD_PAIR), F32), pltpu.VMEM((2, tq, LANES), F32)],
        compiler_params=_cparams(("parallel", "parallel", "arbitrary")),
        name="attn_prompt",
    )(sb_bias, q, kb, vb, _suffix_matrix(tq))


def _attn_sample_kernel(pt_ref, bias_ref, q_ref, kn_ref, vn_ref, us_ref, ex_ref, *rest, n_heads, t_new, pages_per_step):
    k_refs = rest[:pages_per_step]
    v_refs = rest[pages_per_step:2 * pages_per_step]
    o_ref, acc_ref, carry_ref = rest[2 * pages_per_step:]
    step = pl.program_id(1)
    rows = t_new * n_heads
    cols = kn_ref.shape[0]
    page = cols // n_heads
    row_i = lax.broadcasted_iota(jnp.int32, (rows, cols), 0)
    col_i = lax.broadcasted_iota(jnp.int32, (rows, cols), 1)
    own = (row_i % n_heads) == (col_i % n_heads)
    qall = q_ref[...].astype(BF16)
    bias = jnp.concatenate([bias_ref[...]] * (cols // LANES), axis=1)

    def block(k_ref, v_ref, keep):
        z = lax.dot_general(qall, k_ref[...].astype(BF16), (((1,), (1,)), ((), ())), preferred_element_type=F32)
        z = z + bias
        log_rest = jnp.minimum(-z, 0.0) - jnp.log(1.0 + jnp.exp(-jnp.abs(z)))
        log_rest = jnp.where(keep, log_rest, 0.0)
        sums = _dot(log_rest.astype(BF16), us_ref[...])
        carry = carry_ref[...]
        right = jnp.concatenate([carry] * (page // LANES), axis=1)
        passed = jnp.exp(sums[:, :page] + right).astype(BF16)
        spread = _dot(passed, ex_ref[...])
        a = jnp.where(keep, jnp.exp(z + log_rest) * spread, 0.0)
        acc_ref[...] += _dot(a.astype(BF16), v_ref[...].astype(BF16))
        carry_ref[...] = carry + sums[:, page:]

    @pl.when(step == 0)
    def _():
        acc_ref[...] = jnp.zeros_like(acc_ref)
        carry_ref[...] = jnp.zeros_like(carry_ref)
        block(kn_ref, vn_ref, own & ((col_i // n_heads) < (row_i // n_heads)))

    for i in range(pages_per_step):
        block(k_refs[i], v_refs[i], own)

    @pl.when(step == pl.num_programs(1) - 1)
    def _():
        o_ref[...] = acc_ref[...]


def _attn_sample(q, k_new, v_new, cache_k, cache_v, layer, page_table, sb_bias, pages_per_step):
    db, t_new, n_heads, _ = q.shape
    depth, pool_pages, page = cache_k.shape[:3]
    n_pages = page_table.shape[1]
    assert n_pages % pages_per_step == 0 and page % LANES == 0 and t_new <= page
    rows = t_new * n_heads
    cols = page * n_heads
    paged = lambda c: c.reshape(depth, pool_pages, cols, HEAD_DIM)
    pad = ((0, 0), (0, page - t_new), (0, 0), (0, 0))
    new_page = lambda a: jnp.pad(a, pad).reshape(db, cols, HEAD_DIM)
    bias_rows = jnp.broadcast_to(jnp.tile(sb_bias, t_new).reshape(rows, 1).astype(F32), (rows, LANES))
    key_of = lax.broadcasted_iota(jnp.int32, (cols, page + LANES), 0) // n_heads
    out_col = lax.broadcasted_iota(jnp.int32, (cols, page + LANES), 1)
    sum_mat = jnp.where((key_of > out_col) | (out_col >= page), 1.0, 0.0).astype(BF16)
    spread_mat = jnp.where(lax.broadcasted_iota(jnp.int32, (page, cols), 0)
                           == lax.broadcasted_iota(jnp.int32, (page, cols), 1) // n_heads, 1.0, 0.0).astype(BF16)

    def page_spec(i):
        def idx(b, s, pt):
            return (layer, pt[b * n_pages + (n_pages - 1 - (s * pages_per_step + i))], 0, 0)
        return pl.BlockSpec((None, None, cols, HEAD_DIM), idx)

    per_b = lambda n: pl.BlockSpec((None, n, HEAD_DIM), lambda b, s, pt: (b, 0, 0))
    const2 = lambda shape: pl.BlockSpec(shape, lambda b, s, pt: (0, 0))
    grid_spec = pltpu.PrefetchScalarGridSpec(
        num_scalar_prefetch=1,
        grid=(db, n_pages // pages_per_step),
        in_specs=[const2((rows, LANES)), per_b(rows), per_b(cols), per_b(cols),
                  const2((cols, page + LANES)), const2((page, cols))]
                 + [page_spec(i) for i in range(pages_per_step)] * 2,
        out_specs=per_b(rows),
        scratch_shapes=[pltpu.VMEM((rows, HEAD_DIM), F32), pltpu.VMEM((rows, LANES), F32)],
    )
    return pl.pallas_call(
        functools.partial(_attn_sample_kernel, n_heads=n_heads, t_new=t_new, pages_per_step=pages_per_step),
        grid_spec=grid_spec,
        out_shape=jax.ShapeDtypeStruct((db, rows, HEAD_DIM), F32),
        compiler_params=_cparams(("parallel", "arbitrary")),
        name="attn_sample",
    )(page_table.reshape(-1), bias_rows, q.reshape(db, rows, HEAD_DIM), new_page(k_new), new_page(v_new),
      sum_mat, spread_mat, *([paged(cache_k)] * pages_per_step), *([paged(cache_v)] * pages_per_step))


def _merge_kernel(x_ref, p_ref, o_ref, act_ref, gate_ref, wp_ref, ps_ref, wo_ref, wc_ref, wout_ref, lg_ref, lb_ref,
                  y_ref, *, alpha):
    d = x_ref.shape[-1]
    br_a = _dot(p_ref[...], wp_ref[...]) * ps_ref[...]
    br_b = _dot(o_ref[...], wo_ref[...])
    br_c = _dot(act_ref[...], wc_ref[...])
    merged = (gate_ref[:, 0:d].astype(F32) * br_a + gate_ref[:, d:2 * d].astype(F32) * br_b
              + gate_ref[:, 2 * d:3 * d].astype(F32) * br_c)
    y = _dot(merged.astype(BF16), wout_ref[...])
    y_ref[...] = _layernorm(alpha * x_ref[...] + y, lg_ref[...], lb_ref[...])


def _merge(x, p, o, act, gates, w_pool, pool_scale, w_att_o, conv_pw, w_out, ln_g, ln_b, alpha, tm):
    m, d = x.shape
    assert m % tm == 0
    row = lambda a: pl.BlockSpec((tm, a.shape[1]), lambda i: (i, 0))
    full = lambda a: _const_spec(a.shape)
    vec = lambda a: a.reshape(1, d)
    args = (x, p, o, act, gates, w_pool, vec(pool_scale), w_att_o, conv_pw, w_out, vec(ln_g), vec(ln_b))
    return pl.pallas_call(
        functools.partial(_merge_kernel, alpha=alpha),
        grid=(m // tm,),
        in_specs=[row(a) for a in args[:5]] + [full(a) for a in args[5:]],
        out_specs=pl.BlockSpec((tm, d), lambda i: (i, 0)),
        out_shape=jax.ShapeDtypeStruct((m, d), F32),
        compiler_params=_cparams(("parallel",)),
        name="merge",
    )(*args)


def _ffn_kernel(x_ref, wg_ref, wu_ref, wd_ref, lg_ref, lb_ref, y_ref, *, alpha, chunk):
    x = x_ref[...]
    xb = x.astype(BF16)
    d_ff = wg_ref.shape[1]
    acc = None
    for c in range(0, d_ff, chunk):
        n = min(chunk, d_ff - c)
        g = _dot(xb, wg_ref[:, c:c + n])
        u = _dot(xb, wu_ref[:, c:c + n])
        h = (g * jax.nn.sigmoid(g) * u).astype(BF16)
        part = _dot(h, wd_ref[c:c + n, :])
        acc = part if acc is None else acc + part
    y_ref[...] = _layernorm(alpha * x + acc, lg_ref[...], lb_ref[...])


def _ffn(x, wg, wu, wd, ln_g, ln_b, alpha, tm):
    m, d = x.shape
    assert m % tm == 0
    vec = lambda a: a.reshape(1, d)
    return pl.pallas_call(
        functools.partial(_ffn_kernel, alpha=alpha, chunk=512),
        grid=(m // tm,),
        in_specs=[pl.BlockSpec((tm, d), lambda i: (i, 0)), _const_spec(wg.shape), _const_spec(wu.shape),
                  _const_spec(wd.shape), _const_spec((1, d)), _const_spec((1, d))],
        out_specs=pl.BlockSpec((tm, d), lambda i: (i, 0)),
        out_shape=jax.ShapeDtypeStruct((m, d), F32),
        compiler_params=_cparams(("parallel",)),
        name="ffn",
    )(x, wg, wu, wd, vec(ln_g), vec(ln_b))


def _row_tile(m, want):
    t = min(want, m)
    while m % t:
        t //= 2
    return t


def kernel(x_prompt, x_sample, cache_k, cache_v, state_pool, state_conv, page_table, w_in, b_gate, sb_bias, pool_w, pool_scale, w_att_o, conv_dw, conv_b, conv_ln_g, conv_ln_b, conv_pw, w_out, ln1_g, ln1_b, ffn_w_gate, ffn_w_up, ffn_w_down, ln2_g, ln2_b):
    depth = w_in.shape[0]
    batch, seq, d = x_prompt.shape
    db, t_new, _ = x_sample.shape
    n_pages = page_table.shape[1]
    page = cache_k.shape[2]
    past = n_pages * page
    alpha = (2.0 * depth) ** 0.25
    w_att = d // 2
    n_heads = w_att // HEAD_DIM
    c_pool = d // 4
    n_groups = len(POOL_WINDOWS)
    gw = c_pool // n_groups

    xp = x_prompt.reshape(batch * seq, d)
    xs = x_sample.reshape(db * t_new, d)
    mp, ms = xp.shape[0], xs.shape[0]
    tm_p = _row_tile(mp, 512)
    tm_s = _row_tile(ms, 128)
    ts = _row_tile(seq, 512)
    tq = _row_tile(seq, 256)

    outs = {n: [] for n in ("kp", "vp", "pp", "cp", "ks", "vs", "ps", "cs")}
    for l in range(depth):
        w_in_b = w_in[l].astype(BF16)
        w_pool = jnp.zeros((c_pool, d), F32)
        for g in range(n_groups):
            w_pool = w_pool.at[g * gw:(g + 1) * gw, g * (d // n_groups):(g + 1) * (d // n_groups)].set(pool_w[l, g])
        w_pool = w_pool.astype(BF16)
        w_o_b = w_att_o[l].astype(BF16)
        w_c_b = conv_pw[l].astype(BF16)
        w_out_b = w_out[l].astype(BF16)
        wg_b = ffn_w_gate[l].astype(BF16)
        wu_b = ffn_w_up[l].astype(BF16)
        wd_b = ffn_w_down[l].astype(BF16)

        def tail(x, p, o, act, gates, tm):
            x1 = _merge(x, p, o, act, gates, w_pool, pool_scale[l], w_o_b, w_c_b, w_out_b, ln1_g[l], ln1_b[l], alpha, tm)
            return _ffn(x1, wg_b, wu_b, wd_b, ln2_g[l], ln2_b[l], alpha, tm)

        ua, q, k, v, kb, vb, glu, gates = _proj(xp, w_in_b, b_gate[l], tm_p)
        p, act = _branch_prompt(ua, glu, conv_dw[l], conv_b[l], conv_ln_g[l], conv_ln_b[l], batch, seq, ts)
        o = _attn_prompt(q, kb, vb, sb_bias[l], batch, seq, tq)
        xp = tail(xp, p, o, act, gates, tm_p)
        outs["kp"].append(k.reshape(batch, seq, n_heads, HEAD_DIM))
        outs["vp"].append(v.reshape(batch, seq, n_heads, HEAD_DIM))
        outs["pp"].append(ua.reshape(batch, seq, c_pool)[:, seq - POOL_BUF:])
        outs["cp"].append(glu.reshape(batch, seq, c_pool)[:, seq - CONV_BUF:])

        ua, q, k, v, kb, vb, glu, gates = _proj(xs, w_in_b, b_gate[l], tm_s)
        pext = jnp.concatenate([state_pool[l], ua.reshape(db, t_new, c_pool)], axis=1)
        cext = jnp.concatenate([state_conv[l], glu.reshape(db, t_new, c_pool)], axis=1)
        p, act = _branch_sample(pext.transpose(1, 0, 2), cext.transpose(1, 0, 2), conv_dw[l], conv_b[l],
                                conv_ln_g[l], conv_ln_b[l], past, t_new)
        p = p.transpose(1, 0, 2).reshape(ms, c_pool)
        act = act.transpose(1, 0, 2).reshape(ms, c_pool)
        heads = lambda a: a.reshape(db, t_new, n_heads, HEAD_DIM)
        o = _attn_sample(heads(q.astype(F32)), heads(k), heads(v), cache_k, cache_v, l, page_table, sb_bias[l],
                         pages_per_step=_row_tile(n_pages, 8))
        xs = tail(xs, p, o.reshape(ms, w_att).astype(BF16), act, gates, tm_s)
        outs["ks"].append(heads(k))
        outs["vs"].append(heads(v))
        outs["ps"].append(pext[:, -POOL_BUF:])
        outs["cs"].append(cext[:, -CONV_BUF:])

    st = lambda n: jnp.stack(outs[n])
    return (xp.reshape(batch, seq, d), xs.reshape(db, t_new, d), st("kp"), st("vp"), st("pp"), st("cp"),
            st("ks"), st("vs"), st("ps"), st("cs"))
```

```python
import functools

import jax
import jax.numpy as jnp
from jax import lax
from jax.experimental import pallas as pl
from jax.experimental.pallas import tpu as pltpu

HEAD_DIM = 64
HEAD_PAIR = 2 * HEAD_DIM
POOL_WINDOWS = (2, 4, 8, 16)
POOL_BUF = max(POOL_WINDOWS) - 1
POOL_HIST = 16
CONV_W = 31
CONV_BUF = CONV_W - 1
CONV_HIST = 32
N_BRANCH = 3
LN_EPS = 1e-5
LOG2_E = 1.4426950408889634
LANES = 128
VMEM_LIMIT = 56 * 1024 * 1024

F32 = jnp.float32
BF16 = jnp.bfloat16


def _cparams(sem):
    return pltpu.CompilerParams(dimension_semantics=sem, vmem_limit_bytes=VMEM_LIMIT)


def _const_spec(shape):
    nd = len(shape)
    return pl.BlockSpec(shape, lambda *_: (0,) * nd, pipeline_mode=pl.Buffered(1))


def _layernorm(x, g, b):
    mu = jnp.mean(x, axis=-1, keepdims=True)
    xc = x - mu
    var = jnp.mean(xc * xc, axis=-1, keepdims=True)
    return xc * lax.rsqrt(var + LN_EPS) * g + b


def _dot(a, b):
    return jnp.dot(a, b, preferred_element_type=F32)


def _proj_kernel(x_ref, w_ref, wkvt_ref, bg_ref, ua_ref, q_ref, k_ref, v_ref, kb_ref, vb_ref, glu_ref, gate_ref,
                 *, d_pool, d_att, d_conv, d_gate, chunk, kv_block, q_scale):
    x = x_ref[...].astype(BF16)

    def seg(lo, n):
        return _dot(x, w_ref[:, lo:lo + n])

    o = 0
    ua_ref[...] = seg(o, d_pool)
    o += d_pool
    q_ref[...] = (seg(o, d_att) * q_scale).astype(BF16)
    o += d_att
    for f_ref, b_ref in ((k_ref, kb_ref), (v_ref, vb_ref)):
        if kv_block is None:
            kv = seg(o, d_att)
            f_ref[...] = kv
            b_ref[...] = kv.astype(BF16)
        else:
            kvt = lax.dot_general(wkvt_ref[o - d_pool - d_att:o - d_pool, :], x, (((1,), (1,)), ((), ())),
                                  preferred_element_type=F32)
            f_ref[...] = kvt
            for j in range(kvt.shape[1] // kv_block):
                b_ref[j] = kvt[:, j * kv_block:(j + 1) * kv_block].astype(BF16)
        o += d_att
    glu_ref[...] = seg(o, d_conv) * jax.nn.sigmoid(seg(o + d_conv, d_conv))
    o += 2 * d_conv
    for c in range(0, d_gate, chunk):
        n = min(chunk, d_gate - c)
        gate_ref[:, c:c + n] = jax.nn.sigmoid(seg(o + c, n) + bg_ref[:, c:c + n]).astype(BF16)


def _proj(x, w_in, w_kvt, b_gate, tm, q_scale, seq=None, kv_block=None):
    m, d = x.shape
    n_in = w_in.shape[1]
    d_pool = d // 4
    d_att = d // 2
    d_conv = d // 4
    d_gate = N_BRANCH * d
    assert n_in == d_pool + 3 * d_att + 2 * d_conv + d_gate and m % tm == 0
    row = lambda n: pl.BlockSpec((tm, n), lambda i: (i, 0))
    if kv_block is None:
        kv_specs = [row(d_att)] * 4
        kv_shapes = [jax.ShapeDtypeStruct((m, d_att), F32)] * 2 + [jax.ShapeDtypeStruct((m, d_att), BF16)] * 2
    else:
        assert seq % tm == 0 and tm % kv_block == 0
        nt = seq // tm
        f_spec = pl.BlockSpec((None, d_att, tm), lambda i: (i // nt, 0, i % nt))
        b_spec = pl.BlockSpec((None, tm // kv_block, d_att, kv_block), lambda i: (i // nt, i % nt, 0, 0))
        kv_specs = [f_spec, f_spec, b_spec, b_spec]
        kv_shapes = ([jax.ShapeDtypeStruct((m // seq, d_att, seq), F32)] * 2
                     + [jax.ShapeDtypeStruct((m // seq, seq // kv_block, d_att, kv_block), BF16)] * 2)
    kern = functools.partial(_proj_kernel, d_pool=d_pool, d_att=d_att, d_conv=d_conv, d_gate=d_gate, chunk=512,
                             kv_block=kv_block, q_scale=q_scale)
    return pl.pallas_call(
        kern,
        grid=(m // tm,),
        in_specs=[row(d), _const_spec((d, n_in)), _const_spec(w_kvt.shape), _const_spec((1, d_gate))],
        out_specs=[row(d_pool), row(d_att)] + kv_specs + [row(d_conv), row(d_gate)],
        out_shape=[jax.ShapeDtypeStruct((m, d_pool), F32), jax.ShapeDtypeStruct((m, d_att), BF16)] + kv_shapes
                  + [jax.ShapeDtypeStruct((m, d_conv), F32), jax.ShapeDtypeStruct((m, d_gate), BF16)],
        compiler_params=_cparams(("parallel",)),
        name="proj",
    )(x, w_in, w_kvt, b_gate.reshape(1, d_gate))


def _pool_windows(load_shifted, ua, pos, c_pool):
    gw = c_pool // len(POOL_WINDOWS)
    col = lax.broadcasted_iota(jnp.int32, ua.shape, ua.ndim - 1)
    run = ua
    win = jnp.zeros_like(ua)
    width = jnp.zeros(ua.shape, jnp.int32)
    for d in range(1, max(POOL_WINDOWS)):
        run = run + load_shifted(d)
        if d + 1 in POOL_WINDOWS:
            g = POOL_WINDOWS.index(d + 1)
            in_group = (col >= g * gw) & (col < (g + 1) * gw)
            win = jnp.where(in_group, run, win)
            width = jnp.where(in_group, d + 1, width)
    count = jnp.minimum(pos + 1, width).astype(F32)
    return win / count - ua


def _conv_act(load_shifted, dw_ref, cb_ref, lg_ref, lb_ref):
    acc = None
    for i in range(CONV_W):
        term = load_shifted(CONV_BUF - i) * dw_ref[i]
        acc = term if acc is None else acc + term
    cc = _layernorm(acc + cb_ref[...], lg_ref[...], lb_ref[...])
    return cc * jax.nn.sigmoid(cc)


def _branch_prompt_kernel(ua_ref, uah_ref, glu_ref, gluh_ref, dw_ref, cb_ref, lg_ref, lb_ref,
                          p_ref, act_ref, pool_ext, conv_ext, *, ts):
    i = pl.program_id(1)
    c_pool = ua_ref.shape[-1]
    live = (i > 0).astype(F32)
    pool_ext[0:POOL_HIST, :] = uah_ref[...] * live
    pool_ext[POOL_HIST:, :] = ua_ref[...]
    conv_ext[0:CONV_HIST, :] = gluh_ref[...] * live
    conv_ext[CONV_HIST:, :] = glu_ref[...]
    pos = i * ts + lax.broadcasted_iota(jnp.int32, (ts, c_pool), 0)
    p = _pool_windows(lambda d: pool_ext[pl.ds(POOL_HIST - d, ts), :], ua_ref[...], pos, c_pool)
    p_ref[...] = p.astype(BF16)
    act = _conv_act(lambda d: conv_ext[pl.ds(CONV_HIST - d, ts), :], dw_ref, cb_ref, lg_ref, lb_ref)
    act_ref[...] = act.astype(BF16)


def _branch_prompt(ua, glu, conv_dw, conv_b, ln_g, ln_b, batch, seq, ts):
    m, c = ua.shape
    assert seq % ts == 0 and ts % CONV_HIST == 0
    nt = seq // ts
    tile = pl.BlockSpec((ts, c), lambda b, i: (b * nt + i, 0))

    def hist(h):
        per = ts // h
        return pl.BlockSpec((h, c), lambda b, i: (jnp.maximum((b * nt + i) * per - 1, 0), 0))

    vec = _const_spec((1, c))
    return pl.pallas_call(
        functools.partial(_branch_prompt_kernel, ts=ts),
        grid=(batch, nt),
        in_specs=[tile, hist(POOL_HIST), tile, hist(CONV_HIST), _const_spec((CONV_W, 1, c)), vec, vec, vec],
        out_specs=[tile, tile],
        out_shape=[jax.ShapeDtypeStruct((m, c), BF16), jax.ShapeDtypeStruct((m, c), BF16)],
        scratch_shapes=[pltpu.VMEM((POOL_HIST + ts, c), F32), pltpu.VMEM((CONV_HIST + ts, c), F32)],
        compiler_params=_cparams(("parallel", "parallel")),
        name="branch_prompt",
    )(ua, ua, glu, glu, conv_dw.reshape(CONV_W, 1, c), conv_b.reshape(1, c), ln_g.reshape(1, c), ln_b.reshape(1, c))


def _branch_sample_kernel(pext_ref, cext_ref, dw_ref, cb_ref, lg_ref, lb_ref, p_ref, act_ref, *, start_pos, t_new):
    c_pool = pext_ref.shape[-1]
    shape = pext_ref.shape[1:]
    for t in range(t_new):
        pos = jnp.full(shape, start_pos + t, jnp.int32)
        p = _pool_windows(lambda d: pext_ref[POOL_BUF + t - d], pext_ref[POOL_BUF + t], pos, c_pool)
        p_ref[t] = p.astype(BF16)
        act = _conv_act(lambda d: cext_ref[CONV_BUF + t - d], dw_ref, cb_ref, lg_ref, lb_ref)
        act_ref[t] = act.astype(BF16)


def _branch_sample(pext, cext, conv_dw, conv_b, ln_g, ln_b, start_pos, t_new):
    _, db, c = pext.shape
    return pl.pallas_call(
        functools.partial(_branch_sample_kernel, start_pos=start_pos, t_new=t_new),
        out_shape=[jax.ShapeDtypeStruct((t_new, db, c), BF16), jax.ShapeDtypeStruct((t_new, db, c), BF16)],
        name="branch_sample",
    )(pext, cext, conv_dw.reshape(CONV_W, 1, c), conv_b.reshape(1, c), ln_g.reshape(1, c), ln_b.reshape(1, c))


def _suffix_matrix(tk):
    j = lax.broadcasted_iota(jnp.int32, (tk, tk), 0)
    s = lax.broadcasted_iota(jnp.int32, (tk, tk), 1)
    return jnp.where(j > s, 1.0, 0.0).astype(BF16)


def _sb_block(z, mask, u, carry):
    tk = z.shape[1]
    log_rest = jnp.minimum(-z, 0.0) - jnp.log(1.0 + jnp.exp(-jnp.abs(z)))
    if mask is not None:
        log_rest = jnp.where(mask, log_rest, 0.0)
    sums = _dot(log_rest.astype(BF16), u)
    right = jnp.concatenate([carry] * (tk // LANES), axis=1)
    a = jnp.exp(z + log_rest + sums + right)
    if mask is not None:
        a = jnp.where(mask, a, 0.0)
    return a, carry + jnp.sum(log_rest, axis=1, keepdims=True)


NEG_BIG = -1e30


def _attn_prompt_kernel(bias_ref, q_ref, k_ref, v_ref, u_ref, o_ref,
                        acc_ref, carry_ref, logsig_ref, rest_ref, total_ref, *, tq, tk):
    hp = pl.program_id(1)
    qi = pl.program_id(2)
    lane = lax.broadcasted_iota(jnp.int32, (tq, HEAD_PAIR), 1)
    first = lane < HEAD_DIM
    q = q_ref[...]
    zero = jnp.zeros_like(q)
    q_heads = (jnp.where(first, q, zero), jnp.where(first, zero, q))
    bias = (bias_ref[2 * hp] * LOG2_E, bias_ref[2 * hp + 1] * LOG2_E)
    acc_ref[...] = jnp.zeros_like(acc_ref)
    carry_ref[...] = jnp.zeros_like(carry_ref)

    def logits(j, slot, mask):
        kb = k_ref[j]
        for h in range(2):
            z2 = _dot(q_heads[h], kb) + bias[h]
            neg_abs = pltpu.bitcast(pltpu.bitcast(z2, jnp.uint32) | jnp.uint32(0x80000000), F32)
            softplus2 = jnp.maximum(z2, 0.0) + jnp.log2(1.0 + jnp.exp2(neg_abs))
            log_sig = z2 - softplus2
            if mask is not None:
                softplus2 = jnp.where(mask, softplus2, 0.0)
                log_sig = jnp.where(mask, log_sig, NEG_BIG)
            logsig_ref[slot, h] = log_sig
            rest_ref[slot, h] = softplus2.astype(BF16)
            total_ref[slot, h] = jnp.broadcast_to(jnp.sum(softplus2, axis=1, keepdims=True), (tq, LANES))

    def weigh(j, slot):
        vb = v_ref[j]
        for h in range(2):
            sums = _dot(rest_ref[slot, h], u_ref[...])
            carry = carry_ref[h]
            right = jnp.concatenate([carry] * (tk // LANES), axis=1)
            a = jnp.exp2(logsig_ref[slot, h] + sums + right)
            acc_ref[h] += lax.dot_general(a.astype(BF16), vb, (((1,), (1,)), ((), ())), preferred_element_type=F32)
            carry_ref[h] = carry - total_ref[slot, h]

    ratio = tq // tk
    n = (qi + 1) * ratio
    row = lax.broadcasted_iota(jnp.int32, (tq, tk), 0)
    colk = lax.broadcasted_iota(jnp.int32, (tq, tk), 1)
    diag_mask = lambda i: colk + (ratio - 1 - i) * tk < row
    logits(n - 1, 0, diag_mask(0))
    for jj in range(ratio - 1):
        weigh(n - 1 - jj, jj % 2)
        logits(n - 2 - jj, (jj + 1) % 2, diag_mask(jj + 1))

    def body(jj, c):
        weigh(n - 1 - jj, jj % 2)
        logits(n - 2 - jj, (jj + 1) % 2, None)
        return c

    lax.fori_loop(ratio - 1, n - 1, body, 0)
    weigh(0, (n - 1) % 2)
    o_ref[...] = jnp.where(first, acc_ref[0], acc_ref[1]).astype(BF16)


def _attn_prompt(q, kb, vb, sb_bias, batch, seq, tq, tk):
    m, w = q.shape
    assert seq % tq == 0 and tq % tk == 0 and w % HEAD_PAIR == 0 and kb.shape == (batch, seq // tk, w, tk)
    nq = seq // tq
    qspec = pl.BlockSpec((tq, HEAD_PAIR), lambda b, h, i: (b * nq + i, h))
    kvspec = pl.BlockSpec((None, seq // tk, HEAD_PAIR, tk), lambda b, h, i: (b, 0, h, 0))
    return pl.pallas_call(
        functools.partial(_attn_prompt_kernel, tq=tq, tk=tk),
        grid=(batch, w // HEAD_PAIR, nq),
        in_specs=[pl.BlockSpec(memory_space=pltpu.SMEM), qspec, kvspec, kvspec, _const_spec((tk, tk))],
        out_specs=qspec,
        out_shape=jax.ShapeDtypeStruct((m, w), BF16),
        scratch_shapes=[pltpu.VMEM((2, tq, HEAD_PAIR), F32), pltpu.VMEM((2, tq, LANES), F32),
                        pltpu.VMEM((2, 2, tq, tk), F32), pltpu.VMEM((2, 2, tq, tk), BF16),
                        pltpu.VMEM((2, 2, tq, LANES), F32)],
        compiler_params=_cparams(("parallel", "parallel", "arbitrary")),
        name="attn_prompt",
    )(sb_bias, q, kb, vb, -_suffix_matrix(tk))


def _attn_sample_kernel(pt_ref, bias_ref, q_ref, kn_ref, vn_ref, u_ref, *rest, n_heads, t_new, pages_per_step):
    k_refs = rest[:pages_per_step]
    v_refs = rest[pages_per_step:2 * pages_per_step]
    o_ref, qrow_ref, acc_ref, carry_ref = rest[2 * pages_per_step:]
    step = pl.program_id(1)
    rows = t_new * n_heads
    width = n_heads * HEAD_DIM
    page = kn_ref.shape[1]
    own = (lax.broadcasted_iota(jnp.int32, (rows, width), 1) // HEAD_DIM
           == lax.broadcasted_iota(jnp.int32, (rows, width), 0) % n_heads)

    def block(kt, vt, mask):
        z = _dot(qrow_ref[...], kt.astype(BF16)) + bias_ref[...]
        a, carry = _sb_block(z, mask, u_ref[...], carry_ref[...])
        carry_ref[...] = carry
        acc_ref[...] += lax.dot_general(a.astype(BF16), vt.astype(BF16), (((1,), (1,)), ((), ())),
                                        preferred_element_type=F32)

    @pl.when(step == 0)
    def _():
        q = q_ref[...]
        qrep = jnp.broadcast_to(q[:, None, :], (t_new, n_heads, width)).reshape(rows, width)
        qrow_ref[...] = jnp.where(own, qrep, jnp.zeros_like(qrep))
        acc_ref[...] = jnp.zeros_like(acc_ref)
        carry_ref[...] = jnp.zeros_like(carry_ref)
        t_of_row = lax.broadcasted_iota(jnp.int32, (rows, page), 0) // n_heads
        key = lax.broadcasted_iota(jnp.int32, (rows, page), 1)
        block(kn_ref[...], vn_ref[...], key < t_of_row)

    n = pages_per_step
    stack = lambda parts: jnp.concatenate(parts, axis=0)
    kt = jnp.concatenate([r[...].astype(BF16) for r in k_refs], axis=1)
    vt = jnp.concatenate([r[...].astype(BF16) for r in v_refs], axis=1)
    z_wide = _dot(qrow_ref[...], kt)
    z = stack([z_wide[:, i * page:(i + 1) * page] for i in range(n)]) + stack([bias_ref[...]] * n)
    log_rest = jnp.minimum(-z, 0.0) - jnp.log(1.0 + jnp.exp(-jnp.abs(z)))
    sums = _dot(log_rest.astype(BF16), u_ref[...])
    totals = jnp.sum(log_rest, axis=1, keepdims=True)
    carry = carry_ref[...]
    rights = []
    for i in range(n):
        rights.append(carry)
        carry = carry + totals[i * rows:(i + 1) * rows]
    carry_ref[...] = carry
    right = jnp.concatenate([stack(rights)] * (page // LANES), axis=1)
    a = jnp.exp(z + log_rest + sums + right).astype(BF16)
    a_wide = jnp.concatenate([a[i * rows:(i + 1) * rows] for i in range(n)], axis=1)
    acc_ref[...] += lax.dot_general(a_wide, vt, (((1,), (1,)), ((), ())), preferred_element_type=F32)

    @pl.when(step == pl.num_programs(1) - 1)
    def _():
        picked = jnp.where(own, acc_ref[...], 0.0).reshape(t_new, n_heads, width)
        o_ref[...] = jnp.sum(picked, axis=1).astype(BF16)


def _attn_sample(q, k_new, v_new, cache_k, cache_v, layer, page_table, sb_bias, pages_per_step):
    db, t_new, w = q.shape
    n_heads = w // HEAD_DIM
    depth, pool_pages, page = cache_k.shape[:3]
    n_pages = page_table.shape[1]
    assert n_pages % pages_per_step == 0 and page % LANES == 0 and t_new <= page
    rows = t_new * n_heads
    paged = lambda c: c.transpose(0, 1, 3, 4, 2).reshape(depth, pool_pages, w, page)
    new_page = lambda a: jnp.pad(a.transpose(0, 2, 1), ((0, 0), (0, 0), (0, page - t_new)))
    bias_rows = jnp.tile(sb_bias, t_new).reshape(rows, 1).astype(F32)

    def page_spec(i):
        def idx(b, s, pt):
            return (layer, pt[b * n_pages + (n_pages - 1 - (s * pages_per_step + i))], 0, 0)
        return pl.BlockSpec((None, None, w, page), idx)

    per_b = lambda r, c: pl.BlockSpec((None, r, c), lambda b, s, pt: (b, 0, 0))
    const2 = lambda shape: pl.BlockSpec(shape, lambda b, s, pt: (0, 0))
    grid_spec = pltpu.PrefetchScalarGridSpec(
        num_scalar_prefetch=1,
        grid=(db, n_pages // pages_per_step),
        in_specs=[const2((rows, 1)), per_b(t_new, w), per_b(w, page), per_b(w, page), const2((page, page))]
                 + [page_spec(i) for i in range(pages_per_step)] * 2,
        out_specs=per_b(t_new, w),
        scratch_shapes=[pltpu.VMEM((rows, w), BF16), pltpu.VMEM((rows, w), F32), pltpu.VMEM((rows, LANES), F32)],
    )
    return pl.pallas_call(
        functools.partial(_attn_sample_kernel, n_heads=n_heads, t_new=t_new, pages_per_step=pages_per_step),
        grid_spec=grid_spec,
        out_shape=jax.ShapeDtypeStruct((db, t_new, w), BF16),
        compiler_params=_cparams(("parallel", "arbitrary")),
        name="attn_sample",
    )(page_table.reshape(-1), bias_rows, q, new_page(k_new), new_page(v_new), _suffix_matrix(page),
      *([paged(cache_k)] * pages_per_step), *([paged(cache_v)] * pages_per_step))


def _merge_kernel(x_ref, p_ref, o_ref, act_ref, gate_ref, wp_ref, ps_ref, wo_ref, wc_ref, wout_ref, lg_ref, lb_ref,
                  y_ref, *, alpha):
    d = x_ref.shape[-1]
    br_a = _dot(p_ref[...], wp_ref[...]) * ps_ref[...]
    br_b = _dot(o_ref[...], wo_ref[...])
    br_c = _dot(act_ref[...], wc_ref[...])
    merged = (gate_ref[:, 0:d].astype(F32) * br_a + gate_ref[:, d:2 * d].astype(F32) * br_b
              + gate_ref[:, 2 * d:3 * d].astype(F32) * br_c)
    y = _dot(merged.astype(BF16), wout_ref[...])
    y_ref[...] = _layernorm(alpha * x_ref[...] + y, lg_ref[...], lb_ref[...])


def _merge(x, p, o, act, gates, w_pool, pool_scale, w_att_o, conv_pw, w_out, ln_g, ln_b, alpha, tm):
    m, d = x.shape
    assert m % tm == 0
    row = lambda a: pl.BlockSpec((tm, a.shape[1]), lambda i: (i, 0))
    full = lambda a: _const_spec(a.shape)
    vec = lambda a: a.reshape(1, d)
    args = (x, p, o, act, gates, w_pool, vec(pool_scale), w_att_o, conv_pw, w_out, vec(ln_g), vec(ln_b))
    return pl.pallas_call(
        functools.partial(_merge_kernel, alpha=alpha),
        grid=(m // tm,),
        in_specs=[row(a) for a in args[:5]] + [full(a) for a in args[5:]],
        out_specs=pl.BlockSpec((tm, d), lambda i: (i, 0)),
        out_shape=jax.ShapeDtypeStruct((m, d), F32),
        compiler_params=_cparams(("parallel",)),
        name="merge",
    )(*args)


def _ffn_kernel(x_ref, wg_ref, wu_ref, wd_ref, lg_ref, lb_ref, y_ref, *, alpha, chunk):
    x = x_ref[...]
    xb = x.astype(BF16)
    d_ff = wg_ref.shape[1]
    acc = None
    for c in range(0, d_ff, chunk):
        n = min(chunk, d_ff - c)
        g = _dot(xb, wg_ref[:, c:c + n])
        u = _dot(xb, wu_ref[:, c:c + n])
        h = (g * jax.nn.sigmoid(g) * u).astype(BF16)
        part = _dot(h, wd_ref[c:c + n, :])
        acc = part if acc is None else acc + part
    y_ref[...] = _layernorm(alpha * x + acc, lg_ref[...], lb_ref[...])


def _ffn(x, wg, wu, wd, ln_g, ln_b, alpha, tm):
    m, d = x.shape
    assert m % tm == 0
    vec = lambda a: a.reshape(1, d)
    return pl.pallas_call(
        functools.partial(_ffn_kernel, alpha=alpha, chunk=512),
        grid=(m // tm,),
        in_specs=[pl.BlockSpec((tm, d), lambda i: (i, 0)), _const_spec(wg.shape), _const_spec(wu.shape),
                  _const_spec(wd.shape), _const_spec((1, d)), _const_spec((1, d))],
        out_specs=pl.BlockSpec((tm, d), lambda i: (i, 0)),
        out_shape=jax.ShapeDtypeStruct((m, d), F32),
        compiler_params=_cparams(("parallel",)),
        name="ffn",
    )(x, wg, wu, wd, vec(ln_g), vec(ln_b))


def _row_tile(m, want):
    t = min(want, m)
    while m % t:
        t //= 2
    return t


def kernel(x_prompt, x_sample, cache_k, cache_v, state_pool, state_conv, page_table, w_in, b_gate, sb_bias, pool_w, pool_scale, w_att_o, conv_dw, conv_b, conv_ln_g, conv_ln_b, conv_pw, w_out, ln1_g, ln1_b, ffn_w_gate, ffn_w_up, ffn_w_down, ln2_g, ln2_b):
    depth = w_in.shape[0]
    batch, seq, d = x_prompt.shape
    db, t_new, _ = x_sample.shape
    n_pages = page_table.shape[1]
    page = cache_k.shape[2]
    past = n_pages * page
    alpha = (2.0 * depth) ** 0.25
    w_att = d // 2
    n_heads = w_att // HEAD_DIM
    c_pool = d // 4
    n_groups = len(POOL_WINDOWS)
    gw = c_pool // n_groups

    xp = x_prompt.reshape(batch * seq, d)
    xs = x_sample.reshape(db * t_new, d)
    mp, ms = xp.shape[0], xs.shape[0]
    tm_p = _row_tile(mp, 512)
    tm_s = _row_tile(ms, 128)
    ts = _row_tile(seq, 512)
    tk = _row_tile(seq, 256)
    tq = _row_tile(seq, 512)

    outs = {n: [] for n in ("kp", "vp", "pp", "cp", "ks", "vs", "ps", "cs")}
    for l in range(depth):
        w_in_b = w_in[l].astype(BF16)
        w_kvt = w_in_b[:, c_pool + w_att:c_pool + 3 * w_att].T
        w_pool = jnp.zeros((c_pool, d), F32)
        for g in range(n_groups):
            w_pool = w_pool.at[g * gw:(g + 1) * gw, g * (d // n_groups):(g + 1) * (d // n_groups)].set(pool_w[l, g])
        w_pool = w_pool.astype(BF16)
        w_o_b = w_att_o[l].astype(BF16)
        w_c_b = conv_pw[l].astype(BF16)
        w_out_b = w_out[l].astype(BF16)
        wg_b = ffn_w_gate[l].astype(BF16)
        wu_b = ffn_w_up[l].astype(BF16)
        wd_b = ffn_w_down[l].astype(BF16)

        def tail(x, p, o, act, gates, tm):
            x1 = _merge(x, p, o, act, gates, w_pool, pool_scale[l], w_o_b, w_c_b, w_out_b, ln1_g[l], ln1_b[l], alpha, tm)
            return _ffn(x1, wg_b, wu_b, wd_b, ln2_g[l], ln2_b[l], alpha, tm)

        ua, q, kt, vt, ktb, vtb, glu, gates = _proj(xp, w_in_b, w_kvt, b_gate[l], tm_p,
                                                    HEAD_DIM ** -0.5 * LOG2_E, seq=seq, kv_block=tk)
        p, act = _branch_prompt(ua, glu, conv_dw[l], conv_b[l], conv_ln_g[l], conv_ln_b[l], batch, seq, ts)
        o = _attn_prompt(q, ktb, vtb, sb_bias[l], batch, seq, tq, tk)
        xp = tail(xp, p, o, act, gates, tm_p)
        per_head = lambda a: a.reshape(batch, n_heads, HEAD_DIM, seq).transpose(0, 3, 1, 2)
        outs["kp"].append(per_head(kt))
        outs["vp"].append(per_head(vt))
        outs["pp"].append(ua.reshape(batch, seq, c_pool)[:, seq - POOL_BUF:])
        outs["cp"].append(glu.reshape(batch, seq, c_pool)[:, seq - CONV_BUF:])

        ua, q, k, v, kb, vb, glu, gates = _proj(xs, w_in_b, w_kvt, b_gate[l], tm_s, HEAD_DIM ** -0.5)
        pext = jnp.concatenate([state_pool[l], ua.reshape(db, t_new, c_pool)], axis=1)
        cext = jnp.concatenate([state_conv[l], glu.reshape(db, t_new, c_pool)], axis=1)
        p, act = _branch_sample(pext.transpose(1, 0, 2), cext.transpose(1, 0, 2), conv_dw[l], conv_b[l],
                                conv_ln_g[l], conv_ln_b[l], past, t_new)
        p = p.transpose(1, 0, 2).reshape(ms, c_pool)
        act = act.transpose(1, 0, 2).reshape(ms, c_pool)
        per_b = lambda a: a.reshape(db, t_new, w_att)
        heads = lambda a: a.reshape(db, t_new, n_heads, HEAD_DIM)
        o = _attn_sample(per_b(q), per_b(kb), per_b(vb), cache_k, cache_v, l, page_table, sb_bias[l],
                         pages_per_step=_row_tile(n_pages, 8))
        xs = tail(xs, p, o.reshape(ms, w_att), act, gates, tm_s)
        outs["ks"].append(heads(k))
        outs["vs"].append(heads(v))
        outs["ps"].append(pext[:, -POOL_BUF:])
        outs["cs"].append(cext[:, -CONV_BUF:])

    st = lambda n: jnp.stack(outs[n])
    return (xp.reshape(batch, seq, d), xs.reshape(db, t_new, d), st("kp"), st("vp"), st("pp"), st("cp"),
            st("ks"), st("vs"), st("ps"), st("cs"))
```

```python
import functools

import jax
import jax.numpy as jnp
from jax import lax
from jax.experimental import pallas as pl
from jax.experimental.pallas import tpu as pltpu

HEAD_DIM = 64
HEAD_PAIR = 2 * HEAD_DIM
POOL_WINDOWS = (2, 4, 8, 16)
POOL_BUF = max(POOL_WINDOWS) - 1
POOL_HIST = 16
CONV_W = 31
CONV_BUF = CONV_W - 1
CONV_HIST = 32
N_BRANCH = 3
LN_EPS = 1e-5
LOG2_E = 1.4426950408889634
LANES = 128
VMEM_LIMIT = 56 * 1024 * 1024

F32 = jnp.float32
BF16 = jnp.bfloat16


def _cparams(sem):
    return pltpu.CompilerParams(dimension_semantics=sem, vmem_limit_bytes=VMEM_LIMIT)


def _const_spec(shape):
    nd = len(shape)
    return pl.BlockSpec(shape, lambda *_: (0,) * nd, pipeline_mode=pl.Buffered(1))


def _layernorm(x, g, b):
    mu = jnp.mean(x, axis=-1, keepdims=True)
    xc = x - mu
    var = jnp.mean(xc * xc, axis=-1, keepdims=True)
    return xc * lax.rsqrt(var + LN_EPS) * g + b


def _dot(a, b):
    return jnp.dot(a, b, preferred_element_type=F32)


def _proj_kernel(x_ref, w_ref, wkvt_ref, bg_ref, ua_ref, q_ref, k_ref, v_ref, kb_ref, vb_ref, glu_ref, gate_ref,
                 *, d_pool, d_att, d_conv, d_gate, chunk, kv_block, q_scale):
    x = x_ref[...].astype(BF16)

    def seg(lo, n):
        return _dot(x, w_ref[:, lo:lo + n])

    o = 0
    ua_ref[...] = seg(o, d_pool)
    o += d_pool
    q_ref[...] = (seg(o, d_att) * q_scale).astype(BF16)
    o += d_att
    for f_ref, b_ref in ((k_ref, kb_ref), (v_ref, vb_ref)):
        if kv_block is None:
            kv = seg(o, d_att)
            f_ref[...] = kv
            b_ref[...] = kv.astype(BF16)
        else:
            kvt = lax.dot_general(wkvt_ref[o - d_pool - d_att:o - d_pool, :], x, (((1,), (1,)), ((), ())),
                                  preferred_element_type=F32)
            f_ref[...] = kvt
            for j in range(kvt.shape[1] // kv_block):
                b_ref[j] = kvt[:, j * kv_block:(j + 1) * kv_block].astype(BF16)
        o += d_att
    glu_ref[...] = seg(o, d_conv) * jax.nn.sigmoid(seg(o + d_conv, d_conv))
    o += 2 * d_conv
    for c in range(0, d_gate, chunk):
        n = min(chunk, d_gate - c)
        gate_ref[:, c:c + n] = jax.nn.sigmoid(seg(o + c, n) + bg_ref[:, c:c + n]).astype(BF16)


def _proj(x, w_in, w_kvt, b_gate, tm, q_scale, seq=None, kv_block=None):
    m, d = x.shape
    n_in = w_in.shape[1]
    d_pool = d // 4
    d_att = d // 2
    d_conv = d // 4
    d_gate = N_BRANCH * d
    assert n_in == d_pool + 3 * d_att + 2 * d_conv + d_gate and m % tm == 0
    row = lambda n: pl.BlockSpec((tm, n), lambda i: (i, 0))
    if kv_block is None:
        kv_specs = [row(d_att)] * 4
        kv_shapes = [jax.ShapeDtypeStruct((m, d_att), F32)] * 2 + [jax.ShapeDtypeStruct((m, d_att), BF16)] * 2
    else:
        assert seq % tm == 0 and tm % kv_block == 0
        nt = seq // tm
        f_spec = pl.BlockSpec((None, d_att, tm), lambda i: (i // nt, 0, i % nt))
        b_spec = pl.BlockSpec((None, tm // kv_block, d_att, kv_block), lambda i: (i // nt, i % nt, 0, 0))
        kv_specs = [f_spec, f_spec, b_spec, b_spec]
        kv_shapes = ([jax.ShapeDtypeStruct((m // seq, d_att, seq), F32)] * 2
                     + [jax.ShapeDtypeStruct((m // seq, seq // kv_block, d_att, kv_block), BF16)] * 2)
    kern = functools.partial(_proj_kernel, d_pool=d_pool, d_att=d_att, d_conv=d_conv, d_gate=d_gate, chunk=512,
                             kv_block=kv_block, q_scale=q_scale)
    return pl.pallas_call(
        kern,
        grid=(m // tm,),
        in_specs=[row(d), _const_spec((d, n_in)), _const_spec(w_kvt.shape), _const_spec((1, d_gate))],
        out_specs=[row(d_pool), row(d_att)] + kv_specs + [row(d_conv), row(d_gate)],
        out_shape=[jax.ShapeDtypeStruct((m, d_pool), F32), jax.ShapeDtypeStruct((m, d_att), BF16)] + kv_shapes
                  + [jax.ShapeDtypeStruct((m, d_conv), F32), jax.ShapeDtypeStruct((m, d_gate), BF16)],
        compiler_params=_cparams(("parallel",)),
        name="proj",
    )(x, w_in, w_kvt, b_gate.reshape(1, d_gate))


def _pool_windows(load_shifted, ua, pos, c_pool):
    gw = c_pool // len(POOL_WINDOWS)
    col = lax.broadcasted_iota(jnp.int32, ua.shape, ua.ndim - 1)
    run = ua
    win = jnp.zeros_like(ua)
    width = jnp.zeros(ua.shape, jnp.int32)
    for d in range(1, max(POOL_WINDOWS)):
        run = run + load_shifted(d)
        if d + 1 in POOL_WINDOWS:
            g = POOL_WINDOWS.index(d + 1)
            in_group = (col >= g * gw) & (col < (g + 1) * gw)
            win = jnp.where(in_group, run, win)
            width = jnp.where(in_group, d + 1, width)
    count = jnp.minimum(pos + 1, width).astype(F32)
    return win / count - ua


def _conv_act(load_shifted, dw_ref, cb_ref, lg_ref, lb_ref):
    acc = None
    for i in range(CONV_W):
        term = load_shifted(CONV_BUF - i) * dw_ref[i]
        acc = term if acc is None else acc + term
    cc = _layernorm(acc + cb_ref[...], lg_ref[...], lb_ref[...])
    return cc * jax.nn.sigmoid(cc)


SUBLANES = 8


def _phase_rows(hist, ts):
    return ts + SUBLANES * ((hist - 1) // SUBLANES)


def _shifted_loader(ext_ref, phase_ref, hist, ts):
    rows = _phase_rows(hist, ts)
    for s in range(1, SUBLANES):
        phase_ref[s - 1] = ext_ref[pl.ds(s, rows), :]

    def load(d):
        m, s = divmod(hist - d, SUBLANES)
        src = ext_ref if s == 0 else phase_ref.at[s - 1]
        return src[pl.ds(SUBLANES * m, ts), :]

    return load


def _branch_prompt_kernel(ua_ref, uah_ref, glu_ref, gluh_ref, dw_ref, cb_ref, lg_ref, lb_ref,
                          p_ref, act_ref, pool_ext, conv_ext, pool_phase, conv_phase, *, ts):
    i = pl.program_id(1)
    c_pool = ua_ref.shape[-1]
    live = (i > 0).astype(F32)
    pool_ext[0:POOL_HIST, :] = uah_ref[...] * live
    pool_ext[POOL_HIST:, :] = ua_ref[...]
    conv_ext[0:CONV_HIST, :] = gluh_ref[...] * live
    conv_ext[CONV_HIST:, :] = glu_ref[...]
    pos = i * ts + lax.broadcasted_iota(jnp.int32, (ts, c_pool), 0)
    p = _pool_windows(_shifted_loader(pool_ext, pool_phase, POOL_HIST, ts), ua_ref[...], pos, c_pool)
    p_ref[...] = p.astype(BF16)
    act = _conv_act(_shifted_loader(conv_ext, conv_phase, CONV_HIST, ts), dw_ref, cb_ref, lg_ref, lb_ref)
    act_ref[...] = act.astype(BF16)


def _branch_prompt(ua, glu, conv_dw, conv_b, ln_g, ln_b, batch, seq, ts):
    m, c = ua.shape
    assert seq % ts == 0 and ts % CONV_HIST == 0
    nt = seq // ts
    tile = pl.BlockSpec((ts, c), lambda b, i: (b * nt + i, 0))

    def hist(h):
        per = ts // h
        return pl.BlockSpec((h, c), lambda b, i: (jnp.maximum((b * nt + i) * per - 1, 0), 0))

    vec = _const_spec((1, c))
    return pl.pallas_call(
        functools.partial(_branch_prompt_kernel, ts=ts),
        grid=(batch, nt),
        in_specs=[tile, hist(POOL_HIST), tile, hist(CONV_HIST), _const_spec((CONV_W, 1, c)), vec, vec, vec],
        out_specs=[tile, tile],
        out_shape=[jax.ShapeDtypeStruct((m, c), BF16), jax.ShapeDtypeStruct((m, c), BF16)],
        scratch_shapes=[pltpu.VMEM((POOL_HIST + ts, c), F32), pltpu.VMEM((CONV_HIST + ts, c), F32),
                        pltpu.VMEM((SUBLANES - 1, _phase_rows(POOL_HIST, ts), c), F32),
                        pltpu.VMEM((SUBLANES - 1, _phase_rows(CONV_HIST, ts), c), F32)],
        compiler_params=_cparams(("parallel", "parallel")),
        name="branch_prompt",
    )(ua, ua, glu, glu, conv_dw.reshape(CONV_W, 1, c), conv_b.reshape(1, c), ln_g.reshape(1, c), ln_b.reshape(1, c))


def _branch_sample_kernel(pext_ref, cext_ref, dw_ref, cb_ref, lg_ref, lb_ref, p_ref, act_ref, *, start_pos, t_new):
    c_pool = pext_ref.shape[-1]
    shape = pext_ref.shape[1:]
    for t in range(t_new):
        pos = jnp.full(shape, start_pos + t, jnp.int32)
        p = _pool_windows(lambda d: pext_ref[POOL_BUF + t - d], pext_ref[POOL_BUF + t], pos, c_pool)
        p_ref[t] = p.astype(BF16)
        act = _conv_act(lambda d: cext_ref[CONV_BUF + t - d], dw_ref, cb_ref, lg_ref, lb_ref)
        act_ref[t] = act.astype(BF16)


def _branch_sample(pext, cext, conv_dw, conv_b, ln_g, ln_b, start_pos, t_new):
    _, db, c = pext.shape
    return pl.pallas_call(
        functools.partial(_branch_sample_kernel, start_pos=start_pos, t_new=t_new),
        out_shape=[jax.ShapeDtypeStruct((t_new, db, c), BF16), jax.ShapeDtypeStruct((t_new, db, c), BF16)],
        name="branch_sample",
    )(pext, cext, conv_dw.reshape(CONV_W, 1, c), conv_b.reshape(1, c), ln_g.reshape(1, c), ln_b.reshape(1, c))


def _suffix_matrix(tk):
    j = lax.broadcasted_iota(jnp.int32, (tk, tk), 0)
    s = lax.broadcasted_iota(jnp.int32, (tk, tk), 1)
    return jnp.where(j > s, 1.0, 0.0).astype(BF16)


def _sb_block(z, mask, u, carry):
    tk = z.shape[1]
    log_rest = jnp.minimum(-z, 0.0) - jnp.log(1.0 + jnp.exp(-jnp.abs(z)))
    if mask is not None:
        log_rest = jnp.where(mask, log_rest, 0.0)
    sums = _dot(log_rest.astype(BF16), u)
    right = jnp.concatenate([carry] * (tk // LANES), axis=1)
    a = jnp.exp(z + log_rest + sums + right)
    if mask is not None:
        a = jnp.where(mask, a, 0.0)
    return a, carry + jnp.sum(log_rest, axis=1, keepdims=True)


NEG_BIG = -1e30


def _attn_prompt_kernel(bias_ref, q_ref, k_ref, v_ref, u_ref, o_ref,
                        acc_ref, carry_ref, logsig_ref, rest_ref, total_ref, *, tq, tk):
    hp = pl.program_id(1)
    qi = pl.program_id(2)
    lane = lax.broadcasted_iota(jnp.int32, (tq, HEAD_PAIR), 1)
    first = lane < HEAD_DIM
    q = q_ref[...]
    zero = jnp.zeros_like(q)
    q_heads = (jnp.where(first, q, zero), jnp.where(first, zero, q))
    bias = (bias_ref[2 * hp] * LOG2_E, bias_ref[2 * hp + 1] * LOG2_E)
    acc_ref[...] = jnp.zeros_like(acc_ref)
    carry_ref[...] = jnp.zeros_like(carry_ref)

    def logits(j, slot, mask):
        kb = k_ref[j]
        for h in range(2):
            z2 = _dot(q_heads[h], kb) + bias[h]
            neg_abs = pltpu.bitcast(pltpu.bitcast(z2, jnp.uint32) | jnp.uint32(0x80000000), F32)
            softplus2 = jnp.maximum(z2, 0.0) + jnp.log2(1.0 + jnp.exp2(neg_abs))
            log_sig = z2 - softplus2
            if mask is not None:
                softplus2 = jnp.where(mask, softplus2, 0.0)
                log_sig = jnp.where(mask, log_sig, NEG_BIG)
            logsig_ref[slot, h] = log_sig
            rest_ref[slot, h] = softplus2.astype(BF16)
            total_ref[slot, h] = jnp.broadcast_to(jnp.sum(softplus2, axis=1, keepdims=True), (tq, LANES))

    def weigh(j, slot):
        vb = v_ref[j]
        for h in range(2):
            sums = _dot(rest_ref[slot, h], u_ref[...])
            carry = carry_ref[h]
            right = jnp.concatenate([carry] * (tk // LANES), axis=1)
            a = jnp.exp2(logsig_ref[slot, h] + sums + right)
            acc_ref[h] += lax.dot_general(a.astype(BF16), vb, (((1,), (1,)), ((), ())), preferred_element_type=F32)
            carry_ref[h] = carry - total_ref[slot, h]

    ratio = tq // tk
    n = (qi + 1) * ratio
    row = lax.broadcasted_iota(jnp.int32, (tq, tk), 0)
    colk = lax.broadcasted_iota(jnp.int32, (tq, tk), 1)
    diag_mask = lambda i: colk + (ratio - 1 - i) * tk < row
    logits(n - 1, 0, diag_mask(0))
    for jj in range(ratio - 1):
        weigh(n - 1 - jj, jj % 2)
        logits(n - 2 - jj, (jj + 1) % 2, diag_mask(jj + 1))

    assert ratio % 2 == 0
    first_slot = (ratio - 1) % 2

    def body(i, c):
        jj = ratio - 1 + 2 * i
        weigh(n - 1 - jj, first_slot)
        logits(n - 2 - jj, 1 - first_slot, None)
        weigh(n - 2 - jj, 1 - first_slot)
        logits(n - 3 - jj, first_slot, None)
        return c

    lax.fori_loop(0, qi * (ratio // 2), body, 0)
    weigh(0, first_slot)
    o_ref[...] = jnp.where(first, acc_ref[0], acc_ref[1]).astype(BF16)


def _attn_prompt(q, kb, vb, sb_bias, batch, seq, tq, tk):
    m, w = q.shape
    assert seq % tq == 0 and tq % tk == 0 and w % HEAD_PAIR == 0 and kb.shape == (batch, seq // tk, w, tk)
    nq = seq // tq
    qspec = pl.BlockSpec((tq, HEAD_PAIR), lambda b, h, i: (b * nq + i, h))
    kvspec = pl.BlockSpec((None, seq // tk, HEAD_PAIR, tk), lambda b, h, i: (b, 0, h, 0))
    return pl.pallas_call(
        functools.partial(_attn_prompt_kernel, tq=tq, tk=tk),
        grid=(batch, w // HEAD_PAIR, nq),
        in_specs=[pl.BlockSpec(memory_space=pltpu.SMEM), qspec, kvspec, kvspec, _const_spec((tk, tk))],
        out_specs=qspec,
        out_shape=jax.ShapeDtypeStruct((m, w), BF16),
        scratch_shapes=[pltpu.VMEM((2, tq, HEAD_PAIR), F32), pltpu.VMEM((2, tq, LANES), F32),
                        pltpu.VMEM((2, 2, tq, tk), F32), pltpu.VMEM((2, 2, tq, tk), BF16),
                        pltpu.VMEM((2, 2, tq, LANES), F32)],
        compiler_params=_cparams(("parallel", "parallel", "arbitrary")),
        name="attn_prompt",
    )(sb_bias, q, kb, vb, -_suffix_matrix(tk))


def _attn_sample_kernel(pt_ref, bias_ref, q_ref, kn_ref, vn_ref, u_ref, *rest, n_heads, t_new, pages_per_step):
    k_refs = rest[:pages_per_step]
    v_refs = rest[pages_per_step:2 * pages_per_step]
    o_ref, qrow_ref, acc_ref, carry_ref = rest[2 * pages_per_step:]
    step = pl.program_id(1)
    rows = t_new * n_heads
    width = n_heads * HEAD_DIM
    page = kn_ref.shape[1]
    own = (lax.broadcasted_iota(jnp.int32, (rows, width), 1) // HEAD_DIM
           == lax.broadcasted_iota(jnp.int32, (rows, width), 0) % n_heads)

    def block(kt, vt, mask):
        z = _dot(qrow_ref[...], kt.astype(BF16)) + bias_ref[...]
        a, carry = _sb_block(z, mask, u_ref[...], carry_ref[...])
        carry_ref[...] = carry
        acc_ref[...] += lax.dot_general(a.astype(BF16), vt.astype(BF16), (((1,), (1,)), ((), ())),
                                        preferred_element_type=F32)

    @pl.when(step == 0)
    def _():
        q = q_ref[...]
        qrep = jnp.broadcast_to(q[:, None, :], (t_new, n_heads, width)).reshape(rows, width)
        qrow_ref[...] = jnp.where(own, qrep, jnp.zeros_like(qrep))
        acc_ref[...] = jnp.zeros_like(acc_ref)
        carry_ref[...] = jnp.zeros_like(carry_ref)
        t_of_row = lax.broadcasted_iota(jnp.int32, (rows, page), 0) // n_heads
        key = lax.broadcasted_iota(jnp.int32, (rows, page), 1)
        block(kn_ref[...], vn_ref[...], key < t_of_row)

    n = pages_per_step
    stack = lambda parts: jnp.concatenate(parts, axis=0)
    kt = jnp.concatenate([r[...].astype(BF16) for r in k_refs], axis=1)
    vt = jnp.concatenate([r[...].astype(BF16) for r in v_refs], axis=1)
    z_wide = _dot(qrow_ref[...], kt)
    z = stack([z_wide[:, i * page:(i + 1) * page] for i in range(n)]) + stack([bias_ref[...]] * n)
    log_rest = jnp.minimum(-z, 0.0) - jnp.log(1.0 + jnp.exp(-jnp.abs(z)))
    sums = _dot(log_rest.astype(BF16), u_ref[...])
    totals = jnp.sum(log_rest, axis=1, keepdims=True)
    carry = carry_ref[...]
    rights = []
    for i in range(n):
        rights.append(carry)
        carry = carry + totals[i * rows:(i + 1) * rows]
    carry_ref[...] = carry
    right = jnp.concatenate([stack(rights)] * (page // LANES), axis=1)
    a = jnp.exp(z + log_rest + sums + right).astype(BF16)
    a_wide = jnp.concatenate([a[i * rows:(i + 1) * rows] for i in range(n)], axis=1)
    acc_ref[...] += lax.dot_general(a_wide, vt, (((1,), (1,)), ((), ())), preferred_element_type=F32)

    @pl.when(step == pl.num_programs(1) - 1)
    def _():
        picked = jnp.where(own, acc_ref[...], 0.0).reshape(t_new, n_heads, width)
        o_ref[...] = jnp.sum(picked, axis=1).astype(BF16)


def _attn_sample(q, k_new, v_new, cache_k, cache_v, layer, page_table, sb_bias, pages_per_step):
    db, t_new, w = q.shape
    n_heads = w // HEAD_DIM
    depth, pool_pages, page = cache_k.shape[:3]
    n_pages = page_table.shape[1]
    assert n_pages % pages_per_step == 0 and page % LANES == 0 and t_new <= page
    rows = t_new * n_heads
    paged = lambda c: c.transpose(0, 1, 3, 4, 2).reshape(depth, pool_pages, w, page)
    new_page = lambda a: jnp.pad(a.transpose(0, 2, 1), ((0, 0), (0, 0), (0, page - t_new)))
    bias_rows = jnp.tile(sb_bias, t_new).reshape(rows, 1).astype(F32)

    def page_spec(i):
        def idx(b, s, pt):
            return (layer, pt[b * n_pages + (n_pages - 1 - (s * pages_per_step + i))], 0, 0)
        return pl.BlockSpec((None, None, w, page), idx)

    per_b = lambda r, c: pl.BlockSpec((None, r, c), lambda b, s, pt: (b, 0, 0))
    const2 = lambda shape: pl.BlockSpec(shape, lambda b, s, pt: (0, 0))
    grid_spec = pltpu.PrefetchScalarGridSpec(
        num_scalar_prefetch=1,
        grid=(db, n_pages // pages_per_step),
        in_specs=[const2((rows, 1)), per_b(t_new, w), per_b(w, page), per_b(w, page), const2((page, page))]
                 + [page_spec(i) for i in range(pages_per_step)] * 2,
        out_specs=per_b(t_new, w),
        scratch_shapes=[pltpu.VMEM((rows, w), BF16), pltpu.VMEM((rows, w), F32), pltpu.VMEM((rows, LANES), F32)],
    )
    return pl.pallas_call(
        functools.partial(_attn_sample_kernel, n_heads=n_heads, t_new=t_new, pages_per_step=pages_per_step),
        grid_spec=grid_spec,
        out_shape=jax.ShapeDtypeStruct((db, t_new, w), BF16),
        compiler_params=_cparams(("parallel", "arbitrary")),
        name="attn_sample",
    )(page_table.reshape(-1), bias_rows, q, new_page(k_new), new_page(v_new), _suffix_matrix(page),
      *([paged(cache_k)] * pages_per_step), *([paged(cache_v)] * pages_per_step))


def _merge_kernel(x_ref, p_ref, o_ref, act_ref, gate_ref, wp_ref, ps_ref, wo_ref, wc_ref, wout_ref, lg_ref, lb_ref,
                  y_ref, *, alpha):
    d = x_ref.shape[-1]
    br_a = _dot(p_ref[...], wp_ref[...]) * ps_ref[...]
    br_b = _dot(o_ref[...], wo_ref[...])
    br_c = _dot(act_ref[...], wc_ref[...])
    merged = (gate_ref[:, 0:d].astype(F32) * br_a + gate_ref[:, d:2 * d].astype(F32) * br_b
              + gate_ref[:, 2 * d:3 * d].astype(F32) * br_c)
    y = _dot(merged.astype(BF16), wout_ref[...])
    y_ref[...] = _layernorm(alpha * x_ref[...] + y, lg_ref[...], lb_ref[...])


def _merge(x, p, o, act, gates, w_pool, pool_scale, w_att_o, conv_pw, w_out, ln_g, ln_b, alpha, tm):
    m, d = x.shape
    assert m % tm == 0
    row = lambda a: pl.BlockSpec((tm, a.shape[1]), lambda i: (i, 0))
    full = lambda a: _const_spec(a.shape)
    vec = lambda a: a.reshape(1, d)
    args = (x, p, o, act, gates, w_pool, vec(pool_scale), w_att_o, conv_pw, w_out, vec(ln_g), vec(ln_b))
    return pl.pallas_call(
        functools.partial(_merge_kernel, alpha=alpha),
        grid=(m // tm,),
        in_specs=[row(a) for a in args[:5]] + [full(a) for a in args[5:]],
        out_specs=pl.BlockSpec((tm, d), lambda i: (i, 0)),
        out_shape=jax.ShapeDtypeStruct((m, d), F32),
        compiler_params=_cparams(("parallel",)),
        name="merge",
    )(*args)


def _ffn_kernel(x_ref, wg_ref, wu_ref, wd_ref, lg_ref, lb_ref, y_ref, *, alpha, chunk):
    x = x_ref[...]
    xb = x.astype(BF16)
    d_ff = wg_ref.shape[1]
    acc = None
    for c in range(0, d_ff, chunk):
        n = min(chunk, d_ff - c)
        g = _dot(xb, wg_ref[:, c:c + n])
        u = _dot(xb, wu_ref[:, c:c + n])
        h = (g * jax.nn.sigmoid(g) * u).astype(BF16)
        part = _dot(h, wd_ref[c:c + n, :])
        acc = part if acc is None else acc + part
    y_ref[...] = _layernorm(alpha * x + acc, lg_ref[...], lb_ref[...])


def _ffn(x, wg, wu, wd, ln_g, ln_b, alpha, tm):
    m, d = x.shape
    assert m % tm == 0
    vec = lambda a: a.reshape(1, d)
    return pl.pallas_call(
        functools.partial(_ffn_kernel, alpha=alpha, chunk=512),
        grid=(m // tm,),
        in_specs=[pl.BlockSpec((tm, d), lambda i: (i, 0)), _const_spec(wg.shape), _const_spec(wu.shape),
                  _const_spec(wd.shape), _const_spec((1, d)), _const_spec((1, d))],
        out_specs=pl.BlockSpec((tm, d), lambda i: (i, 0)),
        out_shape=jax.ShapeDtypeStruct((m, d), F32),
        compiler_params=_cparams(("parallel",)),
        name="ffn",
    )(x, wg, wu, wd, vec(ln_g), vec(ln_b))


def _row_tile(m, want):
    t = min(want, m)
    while m % t:
        t //= 2
    return t


def kernel(x_prompt, x_sample, cache_k, cache_v, state_pool, state_conv, page_table, w_in, b_gate, sb_bias, pool_w, pool_scale, w_att_o, conv_dw, conv_b, conv_ln_g, conv_ln_b, conv_pw, w_out, ln1_g, ln1_b, ffn_w_gate, ffn_w_up, ffn_w_down, ln2_g, ln2_b):
    depth = w_in.shape[0]
    batch, seq, d = x_prompt.shape
    db, t_new, _ = x_sample.shape
    n_pages = page_table.shape[1]
    page = cache_k.shape[2]
    past = n_pages * page
    alpha = (2.0 * depth) ** 0.25
    w_att = d // 2
    n_heads = w_att // HEAD_DIM
    c_pool = d // 4
    n_groups = len(POOL_WINDOWS)
    gw = c_pool // n_groups

    xp = x_prompt.reshape(batch * seq, d)
    xs = x_sample.reshape(db * t_new, d)
    mp, ms = xp.shape[0], xs.shape[0]
    tm_p = _row_tile(mp, 512)
    tm_s = _row_tile(ms, 128)
    ts = _row_tile(seq, 512)
    tk = _row_tile(seq, 256)
    tq = _row_tile(seq, 512)

    outs = {n: [] for n in ("kp", "vp", "pp", "cp", "ks", "vs", "ps", "cs")}
    for l in range(depth):
        w_in_b = w_in[l].astype(BF16)
        w_kvt = w_in_b[:, c_pool + w_att:c_pool + 3 * w_att].T
        w_pool = jnp.zeros((c_pool, d), F32)
        for g in range(n_groups):
            w_pool = w_pool.at[g * gw:(g + 1) * gw, g * (d // n_groups):(g + 1) * (d // n_groups)].set(pool_w[l, g])
        w_pool = w_pool.astype(BF16)
        w_o_b = w_att_o[l].astype(BF16)
        w_c_b = conv_pw[l].astype(BF16)
        w_out_b = w_out[l].astype(BF16)
        wg_b = ffn_w_gate[l].astype(BF16)
        wu_b = ffn_w_up[l].astype(BF16)
        wd_b = ffn_w_down[l].astype(BF16)

        def tail(x, p, o, act, gates, tm):
            x1 = _merge(x, p, o, act, gates, w_pool, pool_scale[l], w_o_b, w_c_b, w_out_b, ln1_g[l], ln1_b[l], alpha, tm)
            return _ffn(x1, wg_b, wu_b, wd_b, ln2_g[l], ln2_b[l], alpha, tm)

        ua, q, kt, vt, ktb, vtb, glu, gates = _proj(xp, w_in_b, w_kvt, b_gate[l], tm_p,
                                                    HEAD_DIM ** -0.5 * LOG2_E, seq=seq, kv_block=tk)
        p, act = _branch_prompt(ua, glu, conv_dw[l], conv_b[l], conv_ln_g[l], conv_ln_b[l], batch, seq, ts)
        o = _attn_prompt(q, ktb, vtb, sb_bias[l], batch, seq, tq, tk)
        xp = tail(xp, p, o, act, gates, tm_p)
        per_head = lambda a: a.reshape(batch, n_heads, HEAD_DIM, seq).transpose(0, 3, 1, 2)
        outs["kp"].append(per_head(kt))
        outs["vp"].append(per_head(vt))
        outs["pp"].append(ua.reshape(batch, seq, c_pool)[:, seq - POOL_BUF:])
        outs["cp"].append(glu.reshape(batch, seq, c_pool)[:, seq - CONV_BUF:])

        ua, q, k, v, kb, vb, glu, gates = _proj(xs, w_in_b, w_kvt, b_gate[l], tm_s, HEAD_DIM ** -0.5)
        pext = jnp.concatenate([state_pool[l], ua.reshape(db, t_new, c_pool)], axis=1)
        cext = jnp.concatenate([state_conv[l], glu.reshape(db, t_new, c_pool)], axis=1)
        p, act = _branch_sample(pext.transpose(1, 0, 2), cext.transpose(1, 0, 2), conv_dw[l], conv_b[l],
                                conv_ln_g[l], conv_ln_b[l], past, t_new)
        p = p.transpose(1, 0, 2).reshape(ms, c_pool)
        act = act.transpose(1, 0, 2).reshape(ms, c_pool)
        per_b = lambda a: a.reshape(db, t_new, w_att)
        heads = lambda a: a.reshape(db, t_new, n_heads, HEAD_DIM)
        o = _attn_sample(per_b(q), per_b(kb), per_b(vb), cache_k, cache_v, l, page_table, sb_bias[l],
                         pages_per_step=_row_tile(n_pages, 16))
        xs = tail(xs, p, o.reshape(ms, w_att), act, gates, tm_s)
        outs["ks"].append(heads(k))
        outs["vs"].append(heads(v))
        outs["ps"].append(pext[:, -POOL_BUF:])
        outs["cs"].append(cext[:, -CONV_BUF:])

    st = lambda n: jnp.stack(outs[n])
    return (xp.reshape(batch, seq, d), xs.reshape(db, t_new, d), st("kp"), st("vp"), st("pp"), st("cp"),
            st("ks"), st("vs"), st("ps"), st("cs"))
```

```python
import functools

import jax
import jax.numpy as jnp
from jax import lax
from jax.experimental import pallas as pl
from jax.experimental.pallas import tpu as pltpu

HEAD_DIM = 64
HEAD_PAIR = 2 * HEAD_DIM
POOL_WINDOWS = (2, 4, 8, 16)
POOL_BUF = max(POOL_WINDOWS) - 1
POOL_HIST = 16
CONV_W = 31
CONV_BUF = CONV_W - 1
CONV_HIST = 32
N_BRANCH = 3
LN_EPS = 1e-5
LOG2_E = 1.4426950408889634
LANES = 128
VMEM_LIMIT = 56 * 1024 * 1024

F32 = jnp.float32
BF16 = jnp.bfloat16


def _cparams(sem):
    return pltpu.CompilerParams(dimension_semantics=sem, vmem_limit_bytes=VMEM_LIMIT)


def _const_spec(shape):
    nd = len(shape)
    return pl.BlockSpec(shape, lambda *_: (0,) * nd, pipeline_mode=pl.Buffered(1))


def _layernorm(x, g, b):
    mu = jnp.mean(x, axis=-1, keepdims=True)
    xc = x - mu
    var = jnp.mean(xc * xc, axis=-1, keepdims=True)
    return xc * lax.rsqrt(var + LN_EPS) * g + b


def _dot(a, b):
    return jnp.dot(a, b, preferred_element_type=F32)


def _proj_kernel(x_ref, w_ref, wkvt_ref, bg_ref, *refs, d_pool, d_att, d_conv, d_gate, chunk, kv_block, q_scale):
    ua_ref, q_ref, k_ref, v_ref, kb_ref, vb_ref, glu_ref, gate_ref = refs[-8:]
    x = x_ref[...].astype(BF16)

    def seg(lo, n):
        return _dot(x, w_ref[:, lo:lo + n])

    o = 0
    ua_ref[...] = seg(o, d_pool)
    o += d_pool
    q_ref[...] = (seg(o, d_att) * q_scale).astype(BF16)
    o += d_att
    for f_ref, b_ref in ((k_ref, kb_ref), (v_ref, vb_ref)):
        if kv_block is None:
            kv = seg(o, d_att)
            f_ref[...] = kv
            b_ref[...] = kv.astype(BF16)
        else:
            kvt = lax.dot_general(wkvt_ref[o - d_pool - d_att:o - d_pool, :], x, (((1,), (1,)), ((), ())),
                                  preferred_element_type=F32)
            if len(f_ref.shape) == 3:
                for layer_slot in range(f_ref.shape[0]):
                    f_ref[layer_slot] = kvt
            else:
                f_ref[...] = kvt
            for j in range(kvt.shape[1] // kv_block):
                b_ref[j] = kvt[:, j * kv_block:(j + 1) * kv_block].astype(BF16)
        o += d_att
    glu_ref[...] = seg(o, d_conv) * jax.nn.sigmoid(seg(o + d_conv, d_conv))
    o += 2 * d_conv
    for c in range(0, d_gate, chunk):
        n = min(chunk, d_gate - c)
        gate_ref[:, c:c + n] = jax.nn.sigmoid(seg(o + c, n) + bg_ref[:, c:c + n]).astype(BF16)


def _proj(x, w_in, w_kvt, b_gate, tm, q_scale, seq=None, kv_block=None, layer=0, depth=1, kv_all=None):
    m, d = x.shape
    n_in = w_in.shape[1]
    d_pool = d // 4
    d_att = d // 2
    d_conv = d // 4
    d_gate = N_BRANCH * d
    assert n_in == d_pool + 3 * d_att + 2 * d_conv + d_gate and m % tm == 0
    row = lambda n: pl.BlockSpec((tm, n), lambda i: (i, 0))
    if kv_block is None:
        kv_specs = [row(d_att)] * 4
        kv_shapes = [jax.ShapeDtypeStruct((m, d_att), F32)] * 2 + [jax.ShapeDtypeStruct((m, d_att), BF16)] * 2
    else:
        assert seq % tm == 0 and tm % kv_block == 0
        nt = seq // tm
        if layer == 0:
            f_spec = pl.BlockSpec((depth, None, d_att, tm), lambda i: (0, i // nt, 0, i % nt))
        else:
            f_spec = pl.BlockSpec((None, None, d_att, tm), lambda i: (layer, i // nt, 0, i % nt))
        b_spec = pl.BlockSpec((None, tm // kv_block, d_att, kv_block), lambda i: (i // nt, i % nt, 0, 0))
        kv_specs = [f_spec, f_spec, b_spec, b_spec]
        kv_shapes = ([jax.ShapeDtypeStruct((depth, m // seq, d_att, seq), F32)] * 2
                     + [jax.ShapeDtypeStruct((m // seq, seq // kv_block, d_att, kv_block), BF16)] * 2)
    kern = functools.partial(_proj_kernel, d_pool=d_pool, d_att=d_att, d_conv=d_conv, d_gate=d_gate, chunk=512,
                             kv_block=kv_block, q_scale=q_scale)
    in_specs = [row(d), _const_spec((d, n_in)), _const_spec(w_kvt.shape), _const_spec((1, d_gate))]
    args = [x, w_in, w_kvt, b_gate.reshape(1, d_gate)]
    aliases = {}
    if kv_all is not None:
        aliases = {len(args): 2, len(args) + 1: 3}
        in_specs += [pl.BlockSpec(memory_space=pl.ANY)] * 2
        args += list(kv_all)
    return pl.pallas_call(
        kern,
        grid=(m // tm,),
        in_specs=in_specs,
        out_specs=[row(d_pool), row(d_att)] + kv_specs + [row(d_conv), row(d_gate)],
        out_shape=[jax.ShapeDtypeStruct((m, d_pool), F32), jax.ShapeDtypeStruct((m, d_att), BF16)] + kv_shapes
                  + [jax.ShapeDtypeStruct((m, d_conv), F32), jax.ShapeDtypeStruct((m, d_gate), BF16)],
        input_output_aliases=aliases,
        compiler_params=_cparams(("parallel",)),
        name="proj",
    )(*args)


def _pool_windows(load_shifted, ua, pos, c_pool):
    gw = c_pool // len(POOL_WINDOWS)
    col = lax.broadcasted_iota(jnp.int32, ua.shape, ua.ndim - 1)
    run = ua
    win = jnp.zeros_like(ua)
    width = jnp.zeros(ua.shape, jnp.int32)
    for d in range(1, max(POOL_WINDOWS)):
        run = run + load_shifted(d)
        if d + 1 in POOL_WINDOWS:
            g = POOL_WINDOWS.index(d + 1)
            in_group = (col >= g * gw) & (col < (g + 1) * gw)
            win = jnp.where(in_group, run, win)
            width = jnp.where(in_group, d + 1, width)
    count = jnp.minimum(pos + 1, width).astype(F32)
    return win / count - ua


def _conv_act(load_shifted, dw_ref, cb_ref, lg_ref, lb_ref):
    acc = None
    for i in range(CONV_W):
        term = load_shifted(CONV_BUF - i) * dw_ref[i]
        acc = term if acc is None else acc + term
    cc = _layernorm(acc + cb_ref[...], lg_ref[...], lb_ref[...])
    return cc * jax.nn.sigmoid(cc)


SUBLANES = 8


def _phase_rows(hist, ts):
    return ts + SUBLANES * ((hist - 1) // SUBLANES)


def _shifted_loader(ext_ref, phase_ref, hist, ts):
    rows = _phase_rows(hist, ts)
    for s in range(1, SUBLANES):
        phase_ref[s - 1] = ext_ref[pl.ds(s, rows), :]

    def load(d):
        m, s = divmod(hist - d, SUBLANES)
        src = ext_ref if s == 0 else phase_ref.at[s - 1]
        return src[pl.ds(SUBLANES * m, ts), :]

    return load


def _branch_prompt_kernel(ua_ref, uah_ref, glu_ref, gluh_ref, dw_ref, cb_ref, lg_ref, lb_ref,
                          p_ref, act_ref, pool_ext, conv_ext, pool_phase, conv_phase, *, ts):
    i = pl.program_id(1)
    c_pool = ua_ref.shape[-1]
    live = (i > 0).astype(F32)
    pool_ext[0:POOL_HIST, :] = uah_ref[...] * live
    pool_ext[POOL_HIST:, :] = ua_ref[...]
    conv_ext[0:CONV_HIST, :] = gluh_ref[...] * live
    conv_ext[CONV_HIST:, :] = glu_ref[...]
    pos = i * ts + lax.broadcasted_iota(jnp.int32, (ts, c_pool), 0)
    p = _pool_windows(_shifted_loader(pool_ext, pool_phase, POOL_HIST, ts), ua_ref[...], pos, c_pool)
    p_ref[...] = p.astype(BF16)
    act = _conv_act(_shifted_loader(conv_ext, conv_phase, CONV_HIST, ts), dw_ref, cb_ref, lg_ref, lb_ref)
    act_ref[...] = act.astype(BF16)


def _branch_prompt(ua, glu, conv_dw, conv_b, ln_g, ln_b, batch, seq, ts):
    m, c = ua.shape
    assert seq % ts == 0 and ts % CONV_HIST == 0
    nt = seq // ts
    tile = pl.BlockSpec((ts, c), lambda b, i: (b * nt + i, 0))

    def hist(h):
        per = ts // h
        return pl.BlockSpec((h, c), lambda b, i: (jnp.maximum((b * nt + i) * per - 1, 0), 0))

    vec = _const_spec((1, c))
    return pl.pallas_call(
        functools.partial(_branch_prompt_kernel, ts=ts),
        grid=(batch, nt),
        in_specs=[tile, hist(POOL_HIST), tile, hist(CONV_HIST), _const_spec((CONV_W, 1, c)), vec, vec, vec],
        out_specs=[tile, tile],
        out_shape=[jax.ShapeDtypeStruct((m, c), BF16), jax.ShapeDtypeStruct((m, c), BF16)],
        scratch_shapes=[pltpu.VMEM((POOL_HIST + ts, c), F32), pltpu.VMEM((CONV_HIST + ts, c), F32),
                        pltpu.VMEM((SUBLANES - 1, _phase_rows(POOL_HIST, ts), c), F32),
                        pltpu.VMEM((SUBLANES - 1, _phase_rows(CONV_HIST, ts), c), F32)],
        compiler_params=_cparams(("parallel", "parallel")),
        name="branch_prompt",
    )(ua, ua, glu, glu, conv_dw.reshape(CONV_W, 1, c), conv_b.reshape(1, c), ln_g.reshape(1, c), ln_b.reshape(1, c))


def _branch_sample_kernel(pext_ref, cext_ref, dw_ref, cb_ref, lg_ref, lb_ref, p_ref, act_ref, *, start_pos, t_new):
    c_pool = pext_ref.shape[-1]
    shape = pext_ref.shape[1:]
    for t in range(t_new):
        pos = jnp.full(shape, start_pos + t, jnp.int32)
        p = _pool_windows(lambda d: pext_ref[POOL_BUF + t - d], pext_ref[POOL_BUF + t], pos, c_pool)
        p_ref[t] = p.astype(BF16)
        act = _conv_act(lambda d: cext_ref[CONV_BUF + t - d], dw_ref, cb_ref, lg_ref, lb_ref)
        act_ref[t] = act.astype(BF16)


def _branch_sample(pext, cext, conv_dw, conv_b, ln_g, ln_b, start_pos, t_new):
    _, db, c = pext.shape
    return pl.pallas_call(
        functools.partial(_branch_sample_kernel, start_pos=start_pos, t_new=t_new),
        out_shape=[jax.ShapeDtypeStruct((t_new, db, c), BF16), jax.ShapeDtypeStruct((t_new, db, c), BF16)],
        name="branch_sample",
    )(pext, cext, conv_dw.reshape(CONV_W, 1, c), conv_b.reshape(1, c), ln_g.reshape(1, c), ln_b.reshape(1, c))


def _suffix_matrix(tk):
    j = lax.broadcasted_iota(jnp.int32, (tk, tk), 0)
    s = lax.broadcasted_iota(jnp.int32, (tk, tk), 1)
    return jnp.where(j > s, 1.0, 0.0).astype(BF16)


def _sb_block(z, mask, u, carry):
    tk = z.shape[1]
    log_rest = jnp.minimum(-z, 0.0) - jnp.log(1.0 + jnp.exp(-jnp.abs(z)))
    if mask is not None:
        log_rest = jnp.where(mask, log_rest, 0.0)
    sums = _dot(log_rest.astype(BF16), u)
    right = jnp.concatenate([carry] * (tk // LANES), axis=1)
    a = jnp.exp(z + log_rest + sums + right)
    if mask is not None:
        a = jnp.where(mask, a, 0.0)
    return a, carry + jnp.sum(log_rest, axis=1, keepdims=True)


NEG_BIG = -1e30


def _attn_prompt_kernel(bias_ref, q_ref, k_ref, v_ref, u_ref, o_ref,
                        acc_ref, carry_ref, logsig_ref, rest_ref, total_ref, *, tq, tk):
    hp = pl.program_id(1)
    qi = pl.program_id(2)
    lane = lax.broadcasted_iota(jnp.int32, (tq, HEAD_PAIR), 1)
    first = lane < HEAD_DIM
    q = q_ref[...]
    zero = jnp.zeros_like(q)
    q_heads = (jnp.where(first, q, zero), jnp.where(first, zero, q))
    bias = (bias_ref[2 * hp] * LOG2_E, bias_ref[2 * hp + 1] * LOG2_E)
    acc_ref[...] = jnp.zeros_like(acc_ref)
    carry_ref[...] = jnp.zeros_like(carry_ref)

    def logits(j, slot, mask, row0=0):
        kb = k_ref[j]
        for h in range(2):
            z = _dot(q_heads[h][row0:], kb) + bias[h]
            neg_abs = pltpu.bitcast(pltpu.bitcast(z, jnp.uint32) | jnp.uint32(0x80000000), F32)
            softplus = jnp.maximum(z, 0.0) + jnp.log2(1.0 + jnp.exp2(neg_abs))
            log_sig = z - softplus
            if mask is not None:
                softplus = jnp.where(mask[row0:], softplus, 0.0)
                log_sig = jnp.where(mask[row0:], log_sig, NEG_BIG)
            logsig_ref[slot, h, row0:] = log_sig
            rest_ref[slot, h, row0:] = softplus.astype(BF16)
            total_ref[slot, h, row0:] = jnp.broadcast_to(jnp.sum(softplus, axis=1, keepdims=True),
                                                         (tq - row0, LANES))

    def weigh(j, slot, row0=0):
        vb = v_ref[j]
        for h in range(2):
            sums = _dot(rest_ref[slot, h, row0:], u_ref[...])
            carry = carry_ref[h, row0:]
            right = jnp.concatenate([carry] * (tk // LANES), axis=1)
            a = jnp.exp2(logsig_ref[slot, h, row0:] + sums + right)
            acc_ref[h, row0:] += lax.dot_general(a.astype(BF16), vb, (((1,), (1,)), ((), ())),
                                                 preferred_element_type=F32)
            carry_ref[h, row0:] = carry - total_ref[slot, h, row0:]

    ratio = tq // tk
    n = (qi + 1) * ratio
    row = lax.broadcasted_iota(jnp.int32, (tq, tk), 0)
    colk = lax.broadcasted_iota(jnp.int32, (tq, tk), 1)
    diag_mask = lambda i: colk + (ratio - 1 - i) * tk < row
    diag_row0 = lambda i: (ratio - 1 - i) * tk
    logits(n - 1, 0, diag_mask(0), diag_row0(0))
    for jj in range(ratio - 1):
        weigh(n - 1 - jj, jj % 2, diag_row0(jj))
        logits(n - 2 - jj, (jj + 1) % 2, diag_mask(jj + 1), diag_row0(jj + 1))

    assert ratio % 2 == 0
    first_slot = (ratio - 1) % 2

    def pair(i):
        jj = ratio - 1 + 2 * i
        weigh(n - 1 - jj, first_slot)
        logits(n - 2 - jj, 1 - first_slot, None)
        weigh(n - 2 - jj, 1 - first_slot)
        logits(n - 3 - jj, first_slot, None)

    def two_pairs(i, c):
        pair(2 * i)
        pair(2 * i + 1)
        return c

    pairs = qi * (ratio // 2)
    lax.fori_loop(0, pairs // 2, two_pairs, 0)

    @pl.when(pairs % 2 == 1)
    def _():
        pair(pairs - 1)

    weigh(0, first_slot)
    o_ref[...] = jnp.where(first, acc_ref[0], acc_ref[1]).astype(BF16)


def _attn_prompt(q, kb, vb, sb_bias, batch, seq, tq, tk):
    m, w = q.shape
    assert seq % tq == 0 and tq % tk == 0 and w % HEAD_PAIR == 0 and kb.shape == (batch, seq // tk, w, tk)
    nq = seq // tq
    qspec = pl.BlockSpec((tq, HEAD_PAIR), lambda b, h, i: (b * nq + i, h))
    kvspec = pl.BlockSpec((None, seq // tk, HEAD_PAIR, tk), lambda b, h, i: (b, 0, h, 0))
    return pl.pallas_call(
        functools.partial(_attn_prompt_kernel, tq=tq, tk=tk),
        grid=(batch, w // HEAD_PAIR, nq),
        in_specs=[pl.BlockSpec(memory_space=pltpu.SMEM), qspec, kvspec, kvspec, _const_spec((tk, tk))],
        out_specs=qspec,
        out_shape=jax.ShapeDtypeStruct((m, w), BF16),
        scratch_shapes=[pltpu.VMEM((2, tq, HEAD_PAIR), F32), pltpu.VMEM((2, tq, LANES), F32),
                        pltpu.VMEM((2, 2, tq, tk), F32), pltpu.VMEM((2, 2, tq, tk), BF16),
                        pltpu.VMEM((2, 2, tq, LANES), F32)],
        compiler_params=_cparams(("parallel", "parallel", "arbitrary")),
        name="attn_prompt",
    )(sb_bias, q, kb, vb, -_suffix_matrix(tk))


def _attn_sample_kernel(pt_ref, bias_ref, q_ref, kn_ref, vn_ref, u_ref, *rest, n_heads, t_new, pages_per_step):
    k_refs = rest[:pages_per_step]
    v_refs = rest[pages_per_step:2 * pages_per_step]
    o_ref, qrow_ref, acc_ref, carry_ref = rest[2 * pages_per_step:]
    step = pl.program_id(1)
    rows = t_new * n_heads
    width = n_heads * HEAD_DIM
    page = kn_ref.shape[1]
    own = (lax.broadcasted_iota(jnp.int32, (rows, width), 1) // HEAD_DIM
           == lax.broadcasted_iota(jnp.int32, (rows, width), 0) % n_heads)

    def block(kt, vt, mask):
        z = _dot(qrow_ref[...], kt.astype(BF16)) + bias_ref[...]
        a, carry = _sb_block(z, mask, u_ref[...], carry_ref[...])
        carry_ref[...] = carry
        acc_ref[...] += lax.dot_general(a.astype(BF16), vt.astype(BF16), (((1,), (1,)), ((), ())),
                                        preferred_element_type=F32)

    @pl.when(step == 0)
    def _():
        q = q_ref[...]
        qrep = jnp.broadcast_to(q[:, None, :], (t_new, n_heads, width)).reshape(rows, width)
        qrow_ref[...] = jnp.where(own, qrep, jnp.zeros_like(qrep))
        acc_ref[...] = jnp.zeros_like(acc_ref)
        carry_ref[...] = jnp.zeros_like(carry_ref)
        t_of_row = lax.broadcasted_iota(jnp.int32, (rows, page), 0) // n_heads
        key = lax.broadcasted_iota(jnp.int32, (rows, page), 1)
        block(kn_ref[...], vn_ref[...], key < t_of_row)

    n = pages_per_step
    stack = lambda parts: jnp.concatenate(parts, axis=0)
    kt = jnp.concatenate([r[...].astype(BF16) for r in k_refs], axis=1)
    vt = jnp.concatenate([r[...].astype(BF16) for r in v_refs], axis=1)
    z_wide = _dot(qrow_ref[...], kt)
    z = stack([z_wide[:, i * page:(i + 1) * page] for i in range(n)]) + stack([bias_ref[...]] * n)
    log_rest = jnp.minimum(-z, 0.0) - jnp.log(1.0 + jnp.exp(-jnp.abs(z)))
    sums = _dot(log_rest.astype(BF16), u_ref[...])
    totals = jnp.sum(log_rest, axis=1, keepdims=True)
    carry = carry_ref[...]
    rights = []
    for i in range(n):
        rights.append(carry)
        carry = carry + totals[i * rows:(i + 1) * rows]
    carry_ref[...] = carry
    right = jnp.concatenate([stack(rights)] * (page // LANES), axis=1)
    a = jnp.exp(z + log_rest + sums + right).astype(BF16)
    a_wide = jnp.concatenate([a[i * rows:(i + 1) * rows] for i in range(n)], axis=1)
    acc_ref[...] += lax.dot_general(a_wide, vt, (((1,), (1,)), ((), ())), preferred_element_type=F32)

    @pl.when(step == pl.num_programs(1) - 1)
    def _():
        picked = jnp.where(own, acc_ref[...], 0.0).reshape(t_new, n_heads, width)
        o_ref[...] = jnp.sum(picked, axis=1).astype(BF16)


def _attn_sample(q, k_new, v_new, cache_k, cache_v, layer, page_table, sb_bias, pages_per_step):
    db, t_new, w = q.shape
    n_heads = w // HEAD_DIM
    depth, pool_pages, page = cache_k.shape[:3]
    n_pages = page_table.shape[1]
    assert n_pages % pages_per_step == 0 and page % LANES == 0 and t_new <= page
    rows = t_new * n_heads
    paged = lambda c: c.transpose(0, 1, 3, 4, 2).reshape(depth, pool_pages, w, page)
    new_page = lambda a: jnp.pad(a.transpose(0, 2, 1), ((0, 0), (0, 0), (0, page - t_new)))
    bias_rows = jnp.tile(sb_bias, t_new).reshape(rows, 1).astype(F32)

    def page_spec(i):
        def idx(b, s, pt):
            return (layer, pt[b * n_pages + (n_pages - 1 - (s * pages_per_step + i))], 0, 0)
        return pl.BlockSpec((None, None, w, page), idx)

    per_b = lambda r, c: pl.BlockSpec((None, r, c), lambda b, s, pt: (b, 0, 0))
    const2 = lambda shape: pl.BlockSpec(shape, lambda b, s, pt: (0, 0))
    grid_spec = pltpu.PrefetchScalarGridSpec(
        num_scalar_prefetch=1,
        grid=(db, n_pages // pages_per_step),
        in_specs=[const2((rows, 1)), per_b(t_new, w), per_b(w, page), per_b(w, page), const2((page, page))]
                 + [page_spec(i) for i in range(pages_per_step)] * 2,
        out_specs=per_b(t_new, w),
        scratch_shapes=[pltpu.VMEM((rows, w), BF16), pltpu.VMEM((rows, w), F32), pltpu.VMEM((rows, LANES), F32)],
    )
    return pl.pallas_call(
        functools.partial(_attn_sample_kernel, n_heads=n_heads, t_new=t_new, pages_per_step=pages_per_step),
        grid_spec=grid_spec,
        out_shape=jax.ShapeDtypeStruct((db, t_new, w), BF16),
        compiler_params=_cparams(("parallel", "arbitrary")),
        name="attn_sample",
    )(page_table.reshape(-1), bias_rows, q, new_page(k_new), new_page(v_new), _suffix_matrix(page),
      *([paged(cache_k)] * pages_per_step), *([paged(cache_v)] * pages_per_step))


def _merge_kernel(x_ref, p_ref, o_ref, act_ref, gate_ref, wp_ref, ps_ref, wo_ref, wc_ref, wout_ref, lg_ref, lb_ref,
                  y_ref, *, alpha):
    d = x_ref.shape[-1]
    br_a = _dot(p_ref[...], wp_ref[...]) * ps_ref[...]
    br_b = _dot(o_ref[...], wo_ref[...])
    br_c = _dot(act_ref[...], wc_ref[...])
    merged = (gate_ref[:, 0:d].astype(F32) * br_a + gate_ref[:, d:2 * d].astype(F32) * br_b
              + gate_ref[:, 2 * d:3 * d].astype(F32) * br_c)
    y = _dot(merged.astype(BF16), wout_ref[...])
    y_ref[...] = _layernorm(alpha * x_ref[...] + y, lg_ref[...], lb_ref[...])


def _merge(x, p, o, act, gates, w_pool, pool_scale, w_att_o, conv_pw, w_out, ln_g, ln_b, alpha, tm):
    m, d = x.shape
    assert m % tm == 0
    row = lambda a: pl.BlockSpec((tm, a.shape[1]), lambda i: (i, 0))
    full = lambda a: _const_spec(a.shape)
    vec = lambda a: a.reshape(1, d)
    args = (x, p, o, act, gates, w_pool, vec(pool_scale), w_att_o, conv_pw, w_out, vec(ln_g), vec(ln_b))
    return pl.pallas_call(
        functools.partial(_merge_kernel, alpha=alpha),
        grid=(m // tm,),
        in_specs=[row(a) for a in args[:5]] + [full(a) for a in args[5:]],
        out_specs=pl.BlockSpec((tm, d), lambda i: (i, 0)),
        out_shape=jax.ShapeDtypeStruct((m, d), F32),
        compiler_params=_cparams(("parallel",)),
        name="merge",
    )(*args)


def _ffn_kernel(x_ref, wg_ref, wu_ref, wd_ref, lg_ref, lb_ref, y_ref, *, alpha, chunk):
    x = x_ref[...]
    xb = x.astype(BF16)
    d_ff = wg_ref.shape[1]
    acc = None
    for c in range(0, d_ff, chunk):
        n = min(chunk, d_ff - c)
        g = _dot(xb, wg_ref[:, c:c + n])
        u = _dot(xb, wu_ref[:, c:c + n])
        h = (g * jax.nn.sigmoid(g) * u).astype(BF16)
        part = _dot(h, wd_ref[c:c + n, :])
        acc = part if acc is None else acc + part
    y_ref[...] = _layernorm(alpha * x + acc, lg_ref[...], lb_ref[...])


def _ffn(x, wg, wu, wd, ln_g, ln_b, alpha, tm):
    m, d = x.shape
    assert m % tm == 0
    vec = lambda a: a.reshape(1, d)
    return pl.pallas_call(
        functools.partial(_ffn_kernel, alpha=alpha, chunk=512),
        grid=(m // tm,),
        in_specs=[pl.BlockSpec((tm, d), lambda i: (i, 0)), _const_spec(wg.shape), _const_spec(wu.shape),
                  _const_spec(wd.shape), _const_spec((1, d)), _const_spec((1, d))],
        out_specs=pl.BlockSpec((tm, d), lambda i: (i, 0)),
        out_shape=jax.ShapeDtypeStruct((m, d), F32),
        compiler_params=_cparams(("parallel",)),
        name="ffn",
    )(x, wg, wu, wd, vec(ln_g), vec(ln_b))


def _row_tile(m, want):
    t = min(want, m)
    while m % t:
        t //= 2
    return t


def kernel(x_prompt, x_sample, cache_k, cache_v, state_pool, state_conv, page_table, w_in, b_gate, sb_bias, pool_w, pool_scale, w_att_o, conv_dw, conv_b, conv_ln_g, conv_ln_b, conv_pw, w_out, ln1_g, ln1_b, ffn_w_gate, ffn_w_up, ffn_w_down, ln2_g, ln2_b):
    depth = w_in.shape[0]
    batch, seq, d = x_prompt.shape
    db, t_new, _ = x_sample.shape
    n_pages = page_table.shape[1]
    page = cache_k.shape[2]
    past = n_pages * page
    alpha = (2.0 * depth) ** 0.25
    w_att = d // 2
    n_heads = w_att // HEAD_DIM
    c_pool = d // 4
    n_groups = len(POOL_WINDOWS)
    gw = c_pool // n_groups

    xp = x_prompt.reshape(batch * seq, d)
    xs = x_sample.reshape(db * t_new, d)
    mp, ms = xp.shape[0], xs.shape[0]
    tm_p = _row_tile(mp, 512)
    tm_s = _row_tile(ms, 128)
    ts = _row_tile(seq, 512)
    tk = _row_tile(seq, 256)
    tq = _row_tile(seq, 512)

    outs = {n: [] for n in ("pp", "cp", "ks", "vs", "ps", "cs")}
    kv_all = None
    for l in range(depth):
        w_in_b = w_in[l].astype(BF16)
        w_kvt = lax.optimization_barrier(w_in[l, :, c_pool + w_att:c_pool + 3 * w_att]).T.astype(BF16)
        w_pool = jnp.zeros((c_pool, d), F32)
        for g in range(n_groups):
            w_pool = w_pool.at[g * gw:(g + 1) * gw, g * (d // n_groups):(g + 1) * (d // n_groups)].set(pool_w[l, g])
        w_pool = w_pool.astype(BF16)
        w_o_b = w_att_o[l].astype(BF16)
        w_c_b = conv_pw[l].astype(BF16)
        w_out_b = w_out[l].astype(BF16)
        wg_b = ffn_w_gate[l].astype(BF16)
        wu_b = ffn_w_up[l].astype(BF16)
        wd_b = ffn_w_down[l].astype(BF16)

        def tail(x, p, o, act, gates, tm):
            x1 = _merge(x, p, o, act, gates, w_pool, pool_scale[l], w_o_b, w_c_b, w_out_b, ln1_g[l], ln1_b[l], alpha, tm)
            return _ffn(x1, wg_b, wu_b, wd_b, ln2_g[l], ln2_b[l], alpha, tm)

        ua, q, kt, vt, ktb, vtb, glu, gates = _proj(xp, w_in_b, w_kvt, b_gate[l], tm_p, HEAD_DIM ** -0.5 * LOG2_E,
                                                    seq=seq, kv_block=tk, layer=l, depth=depth, kv_all=kv_all)
        kv_all = (kt, vt)
        p, act = _branch_prompt(ua, glu, conv_dw[l], conv_b[l], conv_ln_g[l], conv_ln_b[l], batch, seq, ts)
        o = _attn_prompt(q, ktb, vtb, sb_bias[l], batch, seq, tq, tk)
        xp = tail(xp, p, o, act, gates, tm_p)
        outs["pp"].append(ua.reshape(batch, seq, c_pool)[:, seq - POOL_BUF:])
        outs["cp"].append(glu.reshape(batch, seq, c_pool)[:, seq - CONV_BUF:])

        ua, q, k, v, kb, vb, glu, gates = _proj(xs, w_in_b, w_kvt, b_gate[l], tm_s, HEAD_DIM ** -0.5)
        pext = jnp.concatenate([state_pool[l], ua.reshape(db, t_new, c_pool)], axis=1)
        cext = jnp.concatenate([state_conv[l], glu.reshape(db, t_new, c_pool)], axis=1)
        p, act = _branch_sample(pext.transpose(1, 0, 2), cext.transpose(1, 0, 2), conv_dw[l], conv_b[l],
                                conv_ln_g[l], conv_ln_b[l], past, t_new)
        p = p.transpose(1, 0, 2).reshape(ms, c_pool)
        act = act.transpose(1, 0, 2).reshape(ms, c_pool)
        per_b = lambda a: a.reshape(db, t_new, w_att)
        heads = lambda a: a.reshape(db, t_new, n_heads, HEAD_DIM)
        o = _attn_sample(per_b(q), per_b(kb), per_b(vb), cache_k, cache_v, l, page_table, sb_bias[l],
                         pages_per_step=_row_tile(n_pages, 16))
        xs = tail(xs, p, o.reshape(ms, w_att), act, gates, tm_s)
        outs["ks"].append(heads(k))
        outs["vs"].append(heads(v))
        outs["ps"].append(pext[:, -POOL_BUF:])
        outs["cs"].append(cext[:, -CONV_BUF:])

    st = lambda n: jnp.stack(outs[n])
    per_head = lambda a: a.reshape(depth, batch, n_heads, HEAD_DIM, seq).transpose(0, 1, 4, 2, 3)
    return (xp.reshape(batch, seq, d), xs.reshape(db, t_new, d), per_head(kv_all[0]), per_head(kv_all[1]),
            st("pp"), st("cp"),
            st("ks"), st("vs"), st("ps"), st("cs"))
```

```python
import functools

import jax
import jax.numpy as jnp
from jax import lax
from jax.experimental import pallas as pl
from jax.experimental.pallas import tpu as pltpu

HEAD_DIM = 64
HEAD_PAIR = 2 * HEAD_DIM
POOL_WINDOWS = (2, 4, 8, 16)
POOL_BUF = max(POOL_WINDOWS) - 1
POOL_HIST = 16
CONV_W = 31
CONV_BUF = CONV_W - 1
CONV_HIST = 32
N_BRANCH = 3
LN_EPS = 1e-5
LOG2_E = 1.4426950408889634
LANES = 128
VMEM_LIMIT = 56 * 1024 * 1024

F32 = jnp.float32
BF16 = jnp.bfloat16


def _cparams(sem):
    return pltpu.CompilerParams(dimension_semantics=sem, vmem_limit_bytes=VMEM_LIMIT)


def _const_spec(shape):
    nd = len(shape)
    return pl.BlockSpec(shape, lambda *_: (0,) * nd, pipeline_mode=pl.Buffered(1))


def _layernorm(x, g, b):
    mu = jnp.mean(x, axis=-1, keepdims=True)
    xc = x - mu
    var = jnp.mean(xc * xc, axis=-1, keepdims=True)
    return xc * lax.rsqrt(var + LN_EPS) * g + b


def _dot(a, b):
    return jnp.dot(a, b, preferred_element_type=F32)


def _proj_kernel(x_ref, w_ref, wkvt_ref, bg_ref, dw_ref, cb_ref, lg_ref, lb_ref, *refs,
                 d_pool, d_att, d_conv, d_gate, chunk, kv_block, q_scale, tiles_per_seq):
    if kv_block is None:
        ua_ref, q_ref, k_ref, v_ref, kb_ref, vb_ref, glu_ref, gate_ref = refs[-8:]
    else:
        (ua_ref, q_ref, k_ref, v_ref, kb_ref, vb_ref, glu_ref, gate_ref, p_ref, act_ref,
         pool_ext, conv_ext, pool_phase, conv_phase) = refs[-14:]
        tile = pl.program_id(0) % tiles_per_seq

        @pl.when(tile == 0)
        def _():
            pool_ext[0:POOL_HIST, :] = jnp.zeros((POOL_HIST, d_pool), F32)
            conv_ext[0:CONV_HIST, :] = jnp.zeros((CONV_HIST, d_conv), F32)

    x = x_ref[...].astype(BF16)

    def seg(lo, n):
        return _dot(x, w_ref[:, lo:lo + n])

    o_glu = d_pool + 3 * d_att
    ua = seg(0, d_pool)
    ua_ref[...] = ua
    glu = seg(o_glu, d_conv) * jax.nn.sigmoid(seg(o_glu + d_conv, d_conv))
    glu_ref[...] = glu
    if kv_block is not None:
        tm = ua.shape[0]
        pool_ext[POOL_HIST:, :] = ua
        conv_ext[CONV_HIST:, :] = glu
        pool_load = _shifted_loader(pool_ext, pool_phase, POOL_HIST, tm)
        conv_load = _shifted_loader(conv_ext, conv_phase, CONV_HIST, tm)
        for r0 in range(0, tm, BRANCH_ROWS):
            rows = min(BRANCH_ROWS, tm - r0)
            pos = tile * tm + r0 + lax.broadcasted_iota(jnp.int32, (rows, 1), 0)
            p = _pool_windows(lambda d: pool_load(d, r0, rows), pool_load(0, r0, rows), pos, d_pool)
            p_ref[r0:r0 + rows, :] = p.astype(BF16)
            act = _conv_act(lambda d: conv_load(d, r0, rows), dw_ref, cb_ref, lg_ref, lb_ref)
            act_ref[r0:r0 + rows, :] = act.astype(BF16)
        pool_ext[0:POOL_HIST, :] = pool_ext[tm:tm + POOL_HIST, :]
        conv_ext[0:CONV_HIST, :] = conv_ext[tm:tm + CONV_HIST, :]
    o = d_pool
    q_ref[...] = (seg(o, d_att) * q_scale).astype(BF16)
    o += d_att
    for f_ref, b_ref in ((k_ref, kb_ref), (v_ref, vb_ref)):
        if kv_block is None:
            kv = seg(o, d_att)
            f_ref[...] = kv
            b_ref[...] = kv.astype(BF16)
        else:
            kvt = lax.dot_general(wkvt_ref[o - d_pool - d_att:o - d_pool, :], x, (((1,), (1,)), ((), ())),
                                  preferred_element_type=F32)
            if len(f_ref.shape) == 3:
                for layer_slot in range(f_ref.shape[0]):
                    f_ref[layer_slot] = kvt
            else:
                f_ref[...] = kvt
            for j in range(kvt.shape[1] // kv_block):
                b_ref[j] = kvt[:, j * kv_block:(j + 1) * kv_block].astype(BF16)
        o += d_att
    o += 2 * d_conv
    for c in range(0, d_gate, chunk):
        n = min(chunk, d_gate - c)
        gate_ref[:, c:c + n] = jax.nn.sigmoid(seg(o + c, n) + bg_ref[:, c:c + n]).astype(BF16)


def _proj(x, w_in, w_kvt, b_gate, conv, tm, q_scale, seq=None, kv_block=None, layer=0, depth=1, kv_all=None):
    m, d = x.shape
    n_in = w_in.shape[1]
    d_pool = d // 4
    d_att = d // 2
    d_conv = d // 4
    d_gate = N_BRANCH * d
    assert n_in == d_pool + 3 * d_att + 2 * d_conv + d_gate and m % tm == 0
    row = lambda n: pl.BlockSpec((tm, n), lambda i: (i, 0))
    extra_specs, extra_shapes, scratch, nt = [], [], [], 1
    if kv_block is None:
        kv_specs = [row(d_att)] * 4
        kv_shapes = [jax.ShapeDtypeStruct((m, d_att), F32)] * 2 + [jax.ShapeDtypeStruct((m, d_att), BF16)] * 2
    else:
        assert seq % tm == 0 and tm % kv_block == 0 and tm % CONV_HIST == 0
        nt = seq // tm
        extra_specs = [row(d_pool), row(d_conv)]
        extra_shapes = [jax.ShapeDtypeStruct((m, d_pool), BF16), jax.ShapeDtypeStruct((m, d_conv), BF16)]
        scratch = [pltpu.VMEM((POOL_HIST + tm, d_pool), F32), pltpu.VMEM((CONV_HIST + tm, d_conv), F32),
                   pltpu.VMEM((SUBLANES - 1, _phase_rows(POOL_HIST, tm), d_pool), F32),
                   pltpu.VMEM((SUBLANES - 1, _phase_rows(CONV_HIST, tm), d_conv), F32)]
        if layer == 0:
            f_spec = pl.BlockSpec((depth, None, d_att, tm), lambda i: (0, i // nt, 0, i % nt))
        else:
            f_spec = pl.BlockSpec((None, None, d_att, tm), lambda i: (layer, i // nt, 0, i % nt))
        b_spec = pl.BlockSpec((None, tm // kv_block, d_att, kv_block), lambda i: (i // nt, i % nt, 0, 0))
        kv_specs = [f_spec, f_spec, b_spec, b_spec]
        kv_shapes = ([jax.ShapeDtypeStruct((depth, m // seq, d_att, seq), F32)] * 2
                     + [jax.ShapeDtypeStruct((m // seq, seq // kv_block, d_att, kv_block), BF16)] * 2)
    kern = functools.partial(_proj_kernel, d_pool=d_pool, d_att=d_att, d_conv=d_conv, d_gate=d_gate, chunk=512,
                             kv_block=kv_block, q_scale=q_scale, tiles_per_seq=nt)
    conv_dw, conv_b, ln_g, ln_b = conv
    vec = _const_spec((1, d_conv))
    in_specs = [row(d), _const_spec((d, n_in)), _const_spec(w_kvt.shape), _const_spec((1, d_gate)),
                _const_spec((CONV_W, 1, d_conv)), vec, vec, vec]
    args = [x, w_in, w_kvt, b_gate.reshape(1, d_gate), conv_dw.reshape(CONV_W, 1, d_conv),
            conv_b.reshape(1, d_conv), ln_g.reshape(1, d_conv), ln_b.reshape(1, d_conv)]
    aliases = {}
    if kv_all is not None:
        aliases = {len(args): 2, len(args) + 1: 3}
        in_specs += [pl.BlockSpec(memory_space=pl.ANY)] * 2
        args += list(kv_all)
    return pl.pallas_call(
        kern,
        grid=(m // tm,),
        in_specs=in_specs,
        out_specs=[row(d_pool), row(d_att)] + kv_specs + [row(d_conv), row(d_gate)] + extra_specs,
        out_shape=[jax.ShapeDtypeStruct((m, d_pool), F32), jax.ShapeDtypeStruct((m, d_att), BF16)] + kv_shapes
                  + [jax.ShapeDtypeStruct((m, d_conv), F32), jax.ShapeDtypeStruct((m, d_gate), BF16)] + extra_shapes,
        scratch_shapes=scratch,
        input_output_aliases=aliases,
        compiler_params=_cparams(("arbitrary",)),
        name="proj",
    )(*args)


def _pool_windows(load_shifted, ua, pos, c_pool):
    gw = c_pool // len(POOL_WINDOWS)
    col = lax.broadcasted_iota(jnp.int32, (1, c_pool), 1)
    run = ua
    win = None
    width = jnp.zeros((1, c_pool), jnp.int32)
    for d in range(1, max(POOL_WINDOWS)):
        run = run + load_shifted(d)
        if d + 1 in POOL_WINDOWS:
            g = POOL_WINDOWS.index(d + 1)
            in_group = (col >= g * gw) & (col < (g + 1) * gw)
            win = run if win is None else jnp.where(in_group, run, win)
            width = jnp.where(in_group, d + 1, width)
    count = jnp.minimum(pos + 1, width).astype(F32)
    return win / count - ua


def _conv_act(load_shifted, dw_ref, cb_ref, lg_ref, lb_ref):
    acc = None
    for i in range(CONV_W):
        term = load_shifted(CONV_BUF - i) * dw_ref[i]
        acc = term if acc is None else acc + term
    cc = _layernorm(acc + cb_ref[...], lg_ref[...], lb_ref[...])
    return cc * jax.nn.sigmoid(cc)


SUBLANES = 8
BRANCH_ROWS = 64


def _phase_rows(hist, ts):
    return ts + SUBLANES * ((hist - 1) // SUBLANES)


def _shifted_loader(ext_ref, phase_ref, hist, ts):
    rows = _phase_rows(hist, ts)
    for s in range(1, SUBLANES):
        phase_ref[s - 1] = ext_ref[pl.ds(s, rows), :]

    def load(d, row0, n_rows):
        m, s = divmod(hist - d, SUBLANES)
        src = ext_ref if s == 0 else phase_ref.at[s - 1]
        return src[pl.ds(SUBLANES * m + row0, n_rows), :]

    return load


def _branch_sample_kernel(pext_ref, cext_ref, dw_ref, cb_ref, lg_ref, lb_ref, p_ref, act_ref, *, start_pos, t_new):
    c_pool = pext_ref.shape[-1]
    shape = pext_ref.shape[1:]
    for t in range(t_new):
        pos = jnp.full(shape, start_pos + t, jnp.int32)
        p = _pool_windows(lambda d: pext_ref[POOL_BUF + t - d], pext_ref[POOL_BUF + t], pos, c_pool)
        p_ref[t] = p.astype(BF16)
        act = _conv_act(lambda d: cext_ref[CONV_BUF + t - d], dw_ref, cb_ref, lg_ref, lb_ref)
        act_ref[t] = act.astype(BF16)


def _branch_sample(pext, cext, conv_dw, conv_b, ln_g, ln_b, start_pos, t_new):
    _, db, c = pext.shape
    return pl.pallas_call(
        functools.partial(_branch_sample_kernel, start_pos=start_pos, t_new=t_new),
        out_shape=[jax.ShapeDtypeStruct((t_new, db, c), BF16), jax.ShapeDtypeStruct((t_new, db, c), BF16)],
        name="branch_sample",
    )(pext, cext, conv_dw.reshape(CONV_W, 1, c), conv_b.reshape(1, c), ln_g.reshape(1, c), ln_b.reshape(1, c))


def _suffix_matrix(tk):
    j = lax.broadcasted_iota(jnp.int32, (tk, tk), 0)
    s = lax.broadcasted_iota(jnp.int32, (tk, tk), 1)
    return jnp.where(j > s, 1.0, 0.0).astype(BF16)


def _sb_block(z, mask, u, carry):
    tk = z.shape[1]
    log_rest = jnp.minimum(-z, 0.0) - jnp.log(1.0 + jnp.exp(-jnp.abs(z)))
    if mask is not None:
        log_rest = jnp.where(mask, log_rest, 0.0)
    sums = _dot(log_rest.astype(BF16), u)
    right = jnp.concatenate([carry] * (tk // LANES), axis=1)
    a = jnp.exp(z + log_rest + sums + right)
    if mask is not None:
        a = jnp.where(mask, a, 0.0)
    return a, carry + jnp.sum(log_rest, axis=1, keepdims=True)


NEG_BIG = -1e30


def _attn_prompt_kernel(bias_ref, q_ref, k_ref, v_ref, u_ref, o_ref,
                        acc_ref, carry_ref, logsig_ref, rest_ref, total_ref, *, tq, tk):
    hp = pl.program_id(1)
    qi = pl.program_id(2)
    lane = lax.broadcasted_iota(jnp.int32, (tq, HEAD_PAIR), 1)
    first = lane < HEAD_DIM
    q = q_ref[...]
    zero = jnp.zeros_like(q)
    q_heads = (jnp.where(first, q, zero), jnp.where(first, zero, q))
    bias = (bias_ref[2 * hp] * LOG2_E, bias_ref[2 * hp + 1] * LOG2_E)
    acc_ref[...] = jnp.zeros_like(acc_ref)
    carry_ref[...] = jnp.zeros_like(carry_ref)

    def logits(j, slot, mask, row0=0):
        kb = k_ref[j]
        for h in range(2):
            z = _dot(q_heads[h][row0:], kb) + bias[h]
            neg_abs = pltpu.bitcast(pltpu.bitcast(z, jnp.uint32) | jnp.uint32(0x80000000), F32)
            softplus = jnp.maximum(z, 0.0) + jnp.log2(1.0 + jnp.exp2(neg_abs))
            log_sig = z - softplus
            if mask is not None:
                softplus = jnp.where(mask[row0:], softplus, 0.0)
                log_sig = jnp.where(mask[row0:], log_sig, NEG_BIG)
            logsig_ref[slot, h, row0:] = log_sig
            rest_ref[slot, h, row0:] = softplus.astype(BF16)
            total_ref[slot, h, row0:] = jnp.broadcast_to(jnp.sum(softplus, axis=1, keepdims=True),
                                                         (tq - row0, LANES))

    def weigh(j, slot, row0=0):
        vb = v_ref[j]
        for h in range(2):
            sums = _dot(rest_ref[slot, h, row0:], u_ref[...])
            carry = carry_ref[h, row0:]
            right = jnp.concatenate([carry] * (tk // LANES), axis=1)
            a = jnp.exp2(logsig_ref[slot, h, row0:] + sums + right)
            acc_ref[h, row0:] += lax.dot_general(a.astype(BF16), vb, (((1,), (1,)), ((), ())),
                                                 preferred_element_type=F32)
            carry_ref[h, row0:] = carry - total_ref[slot, h, row0:]

    ratio = tq // tk
    n = (qi + 1) * ratio
    row = lax.broadcasted_iota(jnp.int32, (tq, tk), 0)
    colk = lax.broadcasted_iota(jnp.int32, (tq, tk), 1)
    diag_mask = lambda i: colk + (ratio - 1 - i) * tk < row
    diag_row0 = lambda i: (ratio - 1 - i) * tk
    logits(n - 1, 0, diag_mask(0), diag_row0(0))
    for jj in range(ratio - 1):
        weigh(n - 1 - jj, jj % 2, diag_row0(jj))
        logits(n - 2 - jj, (jj + 1) % 2, diag_mask(jj + 1), diag_row0(jj + 1))

    assert ratio % 2 == 0
    first_slot = (ratio - 1) % 2

    def pair(i):
        jj = ratio - 1 + 2 * i
        weigh(n - 1 - jj, first_slot)
        logits(n - 2 - jj, 1 - first_slot, None)
        weigh(n - 2 - jj, 1 - first_slot)
        logits(n - 3 - jj, first_slot, None)

    def two_pairs(i, c):
        pair(2 * i)
        pair(2 * i + 1)
        return c

    pairs = qi * (ratio // 2)
    lax.fori_loop(0, pairs // 2, two_pairs, 0)

    @pl.when(pairs % 2 == 1)
    def _():
        pair(pairs - 1)

    weigh(0, first_slot)
    o_ref[...] = jnp.where(first, acc_ref[0], acc_ref[1]).astype(BF16)


def _attn_prompt(q, kb, vb, sb_bias, batch, seq, tq, tk):
    m, w = q.shape
    assert seq % tq == 0 and tq % tk == 0 and w % HEAD_PAIR == 0 and kb.shape == (batch, seq // tk, w, tk)
    nq = seq // tq
    qspec = pl.BlockSpec((tq, HEAD_PAIR), lambda b, h, i: (b * nq + i, h))
    kvspec = pl.BlockSpec((None, seq // tk, HEAD_PAIR, tk), lambda b, h, i: (b, 0, h, 0))
    return pl.pallas_call(
        functools.partial(_attn_prompt_kernel, tq=tq, tk=tk),
        grid=(batch, w // HEAD_PAIR, nq),
        in_specs=[pl.BlockSpec(memory_space=pltpu.SMEM), qspec, kvspec, kvspec, _const_spec((tk, tk))],
        out_specs=qspec,
        out_shape=jax.ShapeDtypeStruct((m, w), BF16),
        scratch_shapes=[pltpu.VMEM((2, tq, HEAD_PAIR), F32), pltpu.VMEM((2, tq, LANES), F32),
                        pltpu.VMEM((2, 2, tq, tk), F32), pltpu.VMEM((2, 2, tq, tk), BF16),
                        pltpu.VMEM((2, 2, tq, LANES), F32)],
        compiler_params=_cparams(("parallel", "parallel", "arbitrary")),
        name="attn_prompt",
    )(sb_bias, q, kb, vb, -_suffix_matrix(tk))


def _attn_sample_kernel(pt_ref, bias_ref, q_ref, kn_ref, vn_ref, u_ref, *rest, n_heads, t_new, pages_per_step):
    k_refs = rest[:pages_per_step]
    v_refs = rest[pages_per_step:2 * pages_per_step]
    o_ref, qrow_ref, acc_ref, carry_ref = rest[2 * pages_per_step:]
    step = pl.program_id(1)
    rows = t_new * n_heads
    width = n_heads * HEAD_DIM
    page = kn_ref.shape[1]
    own = (lax.broadcasted_iota(jnp.int32, (rows, width), 1) // HEAD_DIM
           == lax.broadcasted_iota(jnp.int32, (rows, width), 0) % n_heads)

    def block(kt, vt, mask):
        z = _dot(qrow_ref[...], kt.astype(BF16)) + bias_ref[...]
        a, carry = _sb_block(z, mask, u_ref[...], carry_ref[...])
        carry_ref[...] = carry
        acc_ref[...] += lax.dot_general(a.astype(BF16), vt.astype(BF16), (((1,), (1,)), ((), ())),
                                        preferred_element_type=F32)

    @pl.when(step == 0)
    def _():
        q = q_ref[...]
        qrep = jnp.broadcast_to(q[:, None, :], (t_new, n_heads, width)).reshape(rows, width)
        qrow_ref[...] = jnp.where(own, qrep, jnp.zeros_like(qrep))
        acc_ref[...] = jnp.zeros_like(acc_ref)
        carry_ref[...] = jnp.zeros_like(carry_ref)
        t_of_row = lax.broadcasted_iota(jnp.int32, (rows, page), 0) // n_heads
        key = lax.broadcasted_iota(jnp.int32, (rows, page), 1)
        block(kn_ref[...], vn_ref[...], key < t_of_row)

    n = pages_per_step
    stack = lambda parts: jnp.concatenate(parts, axis=0)
    kt = jnp.concatenate([r[...].astype(BF16) for r in k_refs], axis=1)
    vt = jnp.concatenate([r[...].astype(BF16) for r in v_refs], axis=1)
    z_wide = _dot(qrow_ref[...], kt)
    z = stack([z_wide[:, i * page:(i + 1) * page] for i in range(n)]) + stack([bias_ref[...]] * n)
    log_rest = jnp.minimum(-z, 0.0) - jnp.log(1.0 + jnp.exp(-jnp.abs(z)))
    sums = _dot(log_rest.astype(BF16), u_ref[...])
    totals = jnp.sum(log_rest, axis=1, keepdims=True)
    carry = carry_ref[...]
    rights = []
    for i in range(n):
        rights.append(carry)
        carry = carry + totals[i * rows:(i + 1) * rows]
    carry_ref[...] = carry
    right = jnp.concatenate([stack(rights)] * (page // LANES), axis=1)
    a = jnp.exp(z + log_rest + sums + right).astype(BF16)
    a_wide = jnp.concatenate([a[i * rows:(i + 1) * rows] for i in range(n)], axis=1)
    acc_ref[...] += lax.dot_general(a_wide, vt, (((1,), (1,)), ((), ())), preferred_element_type=F32)

    @pl.when(step == pl.num_programs(1) - 1)
    def _():
        picked = jnp.where(own, acc_ref[...], 0.0).reshape(t_new, n_heads, width)
        o_ref[...] = jnp.sum(picked, axis=1).astype(BF16)


def _attn_sample(q, k_new, v_new, cache_k, cache_v, layer, page_table, sb_bias, pages_per_step):
    db, t_new, w = q.shape
    n_heads = w // HEAD_DIM
    depth, pool_pages, page = cache_k.shape[:3]
    n_pages = page_table.shape[1]
    assert n_pages % pages_per_step == 0 and page % LANES == 0 and t_new <= page
    rows = t_new * n_heads
    paged = lambda c: c.transpose(0, 1, 3, 4, 2).reshape(depth, pool_pages, w, page)
    new_page = lambda a: jnp.pad(a.transpose(0, 2, 1), ((0, 0), (0, 0), (0, page - t_new)))
    bias_rows = jnp.tile(sb_bias, t_new).reshape(rows, 1).astype(F32)

    def page_spec(i):
        def idx(b, s, pt):
            return (layer, pt[b * n_pages + (n_pages - 1 - (s * pages_per_step + i))], 0, 0)
        return pl.BlockSpec((None, None, w, page), idx)

    per_b = lambda r, c: pl.BlockSpec((None, r, c), lambda b, s, pt: (b, 0, 0))
    const2 = lambda shape: pl.BlockSpec(shape, lambda b, s, pt: (0, 0))
    grid_spec = pltpu.PrefetchScalarGridSpec(
        num_scalar_prefetch=1,
        grid=(db, n_pages // pages_per_step),
        in_specs=[const2((rows, 1)), per_b(t_new, w), per_b(w, page), per_b(w, page), const2((page, page))]
                 + [page_spec(i) for i in range(pages_per_step)] * 2,
        out_specs=per_b(t_new, w),
        scratch_shapes=[pltpu.VMEM((rows, w), BF16), pltpu.VMEM((rows, w), F32), pltpu.VMEM((rows, LANES), F32)],
    )
    return pl.pallas_call(
        functools.partial(_attn_sample_kernel, n_heads=n_heads, t_new=t_new, pages_per_step=pages_per_step),
        grid_spec=grid_spec,
        out_shape=jax.ShapeDtypeStruct((db, t_new, w), BF16),
        compiler_params=_cparams(("parallel", "arbitrary")),
        name="attn_sample",
    )(page_table.reshape(-1), bias_rows, q, new_page(k_new), new_page(v_new), _suffix_matrix(page),
      *([paged(cache_k)] * pages_per_step), *([paged(cache_v)] * pages_per_step))


def _merge_kernel(x_ref, p_ref, o_ref, act_ref, gate_ref, wp_ref, ps_ref, wo_ref, wc_ref, wout_ref, lg_ref, lb_ref,
                  y_ref, *, alpha):
    d = x_ref.shape[-1]
    br_a = _dot(p_ref[...], wp_ref[...]) * ps_ref[...]
    br_b = _dot(o_ref[...], wo_ref[...])
    br_c = _dot(act_ref[...], wc_ref[...])
    merged = (gate_ref[:, 0:d].astype(F32) * br_a + gate_ref[:, d:2 * d].astype(F32) * br_b
              + gate_ref[:, 2 * d:3 * d].astype(F32) * br_c)
    y = _dot(merged.astype(BF16), wout_ref[...])
    y_ref[...] = _layernorm(alpha * x_ref[...] + y, lg_ref[...], lb_ref[...])


def _merge(x, p, o, act, gates, w_pool, pool_scale, w_att_o, conv_pw, w_out, ln_g, ln_b, alpha, tm):
    m, d = x.shape
    assert m % tm == 0
    row = lambda a: pl.BlockSpec((tm, a.shape[1]), lambda i: (i, 0))
    full = lambda a: _const_spec(a.shape)
    vec = lambda a: a.reshape(1, d)
    args = (x, p, o, act, gates, w_pool, vec(pool_scale), w_att_o, conv_pw, w_out, vec(ln_g), vec(ln_b))
    return pl.pallas_call(
        functools.partial(_merge_kernel, alpha=alpha),
        grid=(m // tm,),
        in_specs=[row(a) for a in args[:5]] + [full(a) for a in args[5:]],
        out_specs=pl.BlockSpec((tm, d), lambda i: (i, 0)),
        out_shape=jax.ShapeDtypeStruct((m, d), F32),
        compiler_params=_cparams(("parallel",)),
        name="merge",
    )(*args)


def _ffn_kernel(x_ref, wg_ref, wu_ref, wd_ref, lg_ref, lb_ref, y_ref, *, alpha, chunk):
    x = x_ref[...]
    xb = x.astype(BF16)
    d_ff = wg_ref.shape[1]
    acc = None
    for c in range(0, d_ff, chunk):
        n = min(chunk, d_ff - c)
        g = _dot(xb, wg_ref[:, c:c + n])
        u = _dot(xb, wu_ref[:, c:c + n])
        h = (g * jax.nn.sigmoid(g) * u).astype(BF16)
        part = _dot(h, wd_ref[c:c + n, :])
        acc = part if acc is None else acc + part
    y_ref[...] = _layernorm(alpha * x + acc, lg_ref[...], lb_ref[...])


def _ffn(x, wg, wu, wd, ln_g, ln_b, alpha, tm):
    m, d = x.shape
    assert m % tm == 0
    vec = lambda a: a.reshape(1, d)
    return pl.pallas_call(
        functools.partial(_ffn_kernel, alpha=alpha, chunk=512),
        grid=(m // tm,),
        in_specs=[pl.BlockSpec((tm, d), lambda i: (i, 0)), _const_spec(wg.shape), _const_spec(wu.shape),
                  _const_spec(wd.shape), _const_spec((1, d)), _const_spec((1, d))],
        out_specs=pl.BlockSpec((tm, d), lambda i: (i, 0)),
        out_shape=jax.ShapeDtypeStruct((m, d), F32),
        compiler_params=_cparams(("parallel",)),
        name="ffn",
    )(x, wg, wu, wd, vec(ln_g), vec(ln_b))


def _row_tile(m, want):
    t = min(want, m)
    while m % t:
        t //= 2
    return t


def kernel(x_prompt, x_sample, cache_k, cache_v, state_pool, state_conv, page_table, w_in, b_gate, sb_bias, pool_w, pool_scale, w_att_o, conv_dw, conv_b, conv_ln_g, conv_ln_b, conv_pw, w_out, ln1_g, ln1_b, ffn_w_gate, ffn_w_up, ffn_w_down, ln2_g, ln2_b):
    depth = w_in.shape[0]
    batch, seq, d = x_prompt.shape
    db, t_new, _ = x_sample.shape
    n_pages = page_table.shape[1]
    page = cache_k.shape[2]
    past = n_pages * page
    alpha = (2.0 * depth) ** 0.25
    w_att = d // 2
    n_heads = w_att // HEAD_DIM
    c_pool = d // 4
    n_groups = len(POOL_WINDOWS)
    gw = c_pool // n_groups

    xp = x_prompt.reshape(batch * seq, d)
    xs = x_sample.reshape(db * t_new, d)
    mp, ms = xp.shape[0], xs.shape[0]
    tm_p = _row_tile(mp, 512)
    tm_s = _row_tile(ms, 128)
    tk = _row_tile(seq, 256)
    tq = _row_tile(seq, 512)

    outs = {n: [] for n in ("pp", "cp", "ks", "vs", "ps", "cs")}
    kv_all = None
    for l in range(depth):
        w_in_b = w_in[l].astype(BF16)
        w_kvt = lax.optimization_barrier(w_in[l, :, c_pool + w_att:c_pool + 3 * w_att]).T.astype(BF16)
        w_pool = jnp.zeros((c_pool, d), F32)
        for g in range(n_groups):
            w_pool = w_pool.at[g * gw:(g + 1) * gw, g * (d // n_groups):(g + 1) * (d // n_groups)].set(pool_w[l, g])
        w_pool = w_pool.astype(BF16)
        w_o_b = w_att_o[l].astype(BF16)
        w_c_b = conv_pw[l].astype(BF16)
        w_out_b = w_out[l].astype(BF16)
        wg_b = ffn_w_gate[l].astype(BF16)
        wu_b = ffn_w_up[l].astype(BF16)
        wd_b = ffn_w_down[l].astype(BF16)

        def tail(x, p, o, act, gates, tm):
            x1 = _merge(x, p, o, act, gates, w_pool, pool_scale[l], w_o_b, w_c_b, w_out_b, ln1_g[l], ln1_b[l], alpha, tm)
            return _ffn(x1, wg_b, wu_b, wd_b, ln2_g[l], ln2_b[l], alpha, tm)

        conv = (conv_dw[l], conv_b[l], conv_ln_g[l], conv_ln_b[l])
        ua, q, kt, vt, ktb, vtb, glu, gates, p, act = _proj(
            xp, w_in_b, w_kvt, b_gate[l], conv, tm_p, HEAD_DIM ** -0.5 * LOG2_E,
            seq=seq, kv_block=tk, layer=l, depth=depth, kv_all=kv_all)
        kv_all = (kt, vt)
        o = _attn_prompt(q, ktb, vtb, sb_bias[l], batch, seq, tq, tk)
        xp = tail(xp, p, o, act, gates, tm_p)
        outs["pp"].append(ua.reshape(batch, seq, c_pool)[:, seq - POOL_BUF:])
        outs["cp"].append(glu.reshape(batch, seq, c_pool)[:, seq - CONV_BUF:])

        ua, q, k, v, kb, vb, glu, gates = _proj(xs, w_in_b, w_kvt, b_gate[l], conv, tm_s, HEAD_DIM ** -0.5)
        pext = jnp.concatenate([state_pool[l], ua.reshape(db, t_new, c_pool)], axis=1)
        cext = jnp.concatenate([state_conv[l], glu.reshape(db, t_new, c_pool)], axis=1)
        p, act = _branch_sample(pext.transpose(1, 0, 2), cext.transpose(1, 0, 2), conv_dw[l], conv_b[l],
                                conv_ln_g[l], conv_ln_b[l], past, t_new)
        p = p.transpose(1, 0, 2).reshape(ms, c_pool)
        act = act.transpose(1, 0, 2).reshape(ms, c_pool)
        per_b = lambda a: a.reshape(db, t_new, w_att)
        heads = lambda a: a.reshape(db, t_new, n_heads, HEAD_DIM)
        o = _attn_sample(per_b(q), per_b(kb), per_b(vb), cache_k, cache_v, l, page_table, sb_bias[l],
                         pages_per_step=_row_tile(n_pages, 16))
        xs = tail(xs, p, o.reshape(ms, w_att), act, gates, tm_s)
        outs["ks"].append(heads(k))
        outs["vs"].append(heads(v))
        outs["ps"].append(pext[:, -POOL_BUF:])
        outs["cs"].append(cext[:, -CONV_BUF:])

    st = lambda n: jnp.stack(outs[n])
    per_head = lambda a: a.reshape(depth, batch, n_heads, HEAD_DIM, seq).transpose(0, 1, 4, 2, 3)
    return (xp.reshape(batch, seq, d), xs.reshape(db, t_new, d), per_head(kv_all[0]), per_head(kv_all[1]),
            st("pp"), st("cp"),
            st("ks"), st("vs"), st("ps"), st("cs"))
```

```python
import functools

import jax
import jax.numpy as jnp
from jax import lax
from jax.experimental import pallas as pl
from jax.experimental.pallas import tpu as pltpu

HEAD_DIM = 64
HEAD_PAIR = 2 * HEAD_DIM
POOL_WINDOWS = (2, 4, 8, 16)
POOL_BUF = max(POOL_WINDOWS) - 1
POOL_HIST = 16
CONV_W = 31
CONV_BUF = CONV_W - 1
CONV_HIST = 32
N_BRANCH = 3
LN_EPS = 1e-5
LOG2_E = 1.4426950408889634
LANES = 128
VMEM_LIMIT = 56 * 1024 * 1024

F32 = jnp.float32
BF16 = jnp.bfloat16


def _cparams(sem):
    return pltpu.CompilerParams(dimension_semantics=sem, vmem_limit_bytes=VMEM_LIMIT)


def _const_spec(shape):
    nd = len(shape)
    return pl.BlockSpec(shape, lambda *_: (0,) * nd, pipeline_mode=pl.Buffered(1))


def _layernorm(x, g, b):
    mu = jnp.mean(x, axis=-1, keepdims=True)
    xc = x - mu
    var = jnp.mean(xc * xc, axis=-1, keepdims=True)
    return xc * lax.rsqrt(var + LN_EPS) * g + b


def _dot(a, b):
    return jnp.dot(a, b, preferred_element_type=F32)


def _proj_kernel(x_ref, w_ref, wkvt_ref, bg_ref, dw_ref, cb_ref, lg_ref, lb_ref, *refs,
                 d_pool, d_att, d_conv, d_gate, chunk, kv_block, q_scale, tiles_per_seq):
    if kv_block is None:
        ua_ref, q_ref, k_ref, v_ref, kb_ref, vb_ref, glu_ref, gate_ref = refs[-8:]
    else:
        (ua_ref, q_ref, k_ref, v_ref, kb_ref, vb_ref, glu_ref, gate_ref, p_ref, act_ref,
         pool_ext, conv_ext, pool_phase, conv_phase) = refs[-14:]
        tile = pl.program_id(0) % tiles_per_seq

        @pl.when(tile == 0)
        def _():
            pool_ext[0:POOL_HIST, :] = jnp.zeros((POOL_HIST, d_pool), F32)
            conv_ext[0:CONV_HIST, :] = jnp.zeros((CONV_HIST, d_conv), F32)

    x = x_ref[...].astype(BF16)

    def seg(lo, n):
        return _dot(x, w_ref[:, lo:lo + n].astype(BF16))

    o_glu = d_pool + 3 * d_att
    ua = seg(0, d_pool)
    ua_ref[...] = ua
    glu = seg(o_glu, d_conv) * jax.nn.sigmoid(seg(o_glu + d_conv, d_conv))
    glu_ref[...] = glu
    if kv_block is not None:
        tm = ua.shape[0]
        pool_ext[POOL_HIST:, :] = ua
        conv_ext[CONV_HIST:, :] = glu
        pool_load = _shifted_loader(pool_ext, pool_phase, POOL_HIST, tm)
        conv_load = _shifted_loader(conv_ext, conv_phase, CONV_HIST, tm)
        for r0 in range(0, tm, BRANCH_ROWS):
            rows = min(BRANCH_ROWS, tm - r0)
            pos = tile * tm + r0 + lax.broadcasted_iota(jnp.int32, (rows, 1), 0)
            p = _pool_windows(lambda d: pool_load(d, r0, rows), pool_load(0, r0, rows), pos, d_pool)
            p_ref[r0:r0 + rows, :] = p.astype(BF16)
            act = _conv_act(lambda d: conv_load(d, r0, rows), dw_ref, cb_ref, lg_ref, lb_ref)
            act_ref[r0:r0 + rows, :] = act.astype(BF16)
        pool_ext[0:POOL_HIST, :] = pool_ext[tm:tm + POOL_HIST, :]
        conv_ext[0:CONV_HIST, :] = conv_ext[tm:tm + CONV_HIST, :]
    o = d_pool
    q_ref[...] = (seg(o, d_att) * q_scale).astype(BF16)
    o += d_att
    for f_ref, b_ref in ((k_ref, kb_ref), (v_ref, vb_ref)):
        if kv_block is None:
            kv = seg(o, d_att)
            f_ref[...] = kv
            b_ref[...] = kv.astype(BF16)
        else:
            kvt = lax.dot_general(wkvt_ref[o - d_pool - d_att:o - d_pool, :], x, (((1,), (1,)), ((), ())),
                                  preferred_element_type=F32)
            if len(f_ref.shape) == 3:
                for layer_slot in range(f_ref.shape[0]):
                    f_ref[layer_slot] = kvt
            else:
                f_ref[...] = kvt
            for j in range(kvt.shape[1] // kv_block):
                b_ref[j] = kvt[:, j * kv_block:(j + 1) * kv_block].astype(BF16)
        o += d_att
    o += 2 * d_conv
    for c in range(0, d_gate, chunk):
        n = min(chunk, d_gate - c)
        gate_ref[:, c:c + n] = jax.nn.sigmoid(seg(o + c, n) + bg_ref[:, c:c + n]).astype(BF16)


def _proj(x, w_in, w_kvt, b_gate, conv, tm, q_scale, seq=None, kv_block=None, layer=0, depth=1, kv_all=None):
    m, d = x.shape
    n_in = w_in.shape[1]
    d_pool = d // 4
    d_att = d // 2
    d_conv = d // 4
    d_gate = N_BRANCH * d
    assert n_in == d_pool + 3 * d_att + 2 * d_conv + d_gate and m % tm == 0
    row = lambda n: pl.BlockSpec((tm, n), lambda i: (i, 0))
    extra_specs, extra_shapes, scratch, nt = [], [], [], 1
    if kv_block is None:
        kv_specs = [row(d_att)] * 4
        kv_shapes = [jax.ShapeDtypeStruct((m, d_att), F32)] * 2 + [jax.ShapeDtypeStruct((m, d_att), BF16)] * 2
    else:
        assert seq % tm == 0 and tm % kv_block == 0 and tm % CONV_HIST == 0
        nt = seq // tm
        extra_specs = [row(d_pool), row(d_conv)]
        extra_shapes = [jax.ShapeDtypeStruct((m, d_pool), BF16), jax.ShapeDtypeStruct((m, d_conv), BF16)]
        scratch = [pltpu.VMEM((POOL_HIST + tm, d_pool), F32), pltpu.VMEM((CONV_HIST + tm, d_conv), F32),
                   pltpu.VMEM((SUBLANES - 1, _phase_rows(POOL_HIST, tm), d_pool), F32),
                   pltpu.VMEM((SUBLANES - 1, _phase_rows(CONV_HIST, tm), d_conv), F32)]
        if layer == 0:
            f_spec = pl.BlockSpec((depth, None, d_att, tm), lambda i: (0, i // nt, 0, i % nt))
        else:
            f_spec = pl.BlockSpec((None, None, d_att, tm), lambda i: (layer, i // nt, 0, i % nt))
        b_spec = pl.BlockSpec((None, tm // kv_block, d_att, kv_block), lambda i: (i // nt, i % nt, 0, 0))
        kv_specs = [f_spec, f_spec, b_spec, b_spec]
        kv_shapes = ([jax.ShapeDtypeStruct((depth, m // seq, d_att, seq), F32)] * 2
                     + [jax.ShapeDtypeStruct((m // seq, seq // kv_block, d_att, kv_block), BF16)] * 2)
    kern = functools.partial(_proj_kernel, d_pool=d_pool, d_att=d_att, d_conv=d_conv, d_gate=d_gate, chunk=512,
                             kv_block=kv_block, q_scale=q_scale, tiles_per_seq=nt)
    conv_dw, conv_b, ln_g, ln_b = conv
    vec = _const_spec((1, d_conv))
    in_specs = [row(d), _const_spec((d, n_in)), _const_spec(w_kvt.shape), _const_spec((1, d_gate)),
                _const_spec((CONV_W, 1, d_conv)), vec, vec, vec]
    args = [x, w_in, w_kvt, b_gate.reshape(1, d_gate), conv_dw.reshape(CONV_W, 1, d_conv),
            conv_b.reshape(1, d_conv), ln_g.reshape(1, d_conv), ln_b.reshape(1, d_conv)]
    aliases = {}
    if kv_all is not None:
        aliases = {len(args): 2, len(args) + 1: 3}
        in_specs += [pl.BlockSpec(memory_space=pl.ANY)] * 2
        args += list(kv_all)
    return pl.pallas_call(
        kern,
        grid=(m // tm,),
        in_specs=in_specs,
        out_specs=[row(d_pool), row(d_att)] + kv_specs + [row(d_conv), row(d_gate)] + extra_specs,
        out_shape=[jax.ShapeDtypeStruct((m, d_pool), F32), jax.ShapeDtypeStruct((m, d_att), BF16)] + kv_shapes
                  + [jax.ShapeDtypeStruct((m, d_conv), F32), jax.ShapeDtypeStruct((m, d_gate), BF16)] + extra_shapes,
        scratch_shapes=scratch,
        input_output_aliases=aliases,
        compiler_params=_cparams(("arbitrary",)),
        name="proj",
    )(*args)


def _pool_windows(load_shifted, ua, pos, c_pool):
    gw = c_pool // len(POOL_WINDOWS)
    col = lax.broadcasted_iota(jnp.int32, (1, c_pool), 1)
    run = ua
    win = None
    width = jnp.zeros((1, c_pool), jnp.int32)
    for d in range(1, max(POOL_WINDOWS)):
        run = run + load_shifted(d)
        if d + 1 in POOL_WINDOWS:
            g = POOL_WINDOWS.index(d + 1)
            in_group = (col >= g * gw) & (col < (g + 1) * gw)
            win = run if win is None else jnp.where(in_group, run, win)
            width = jnp.where(in_group, d + 1, width)
    count = jnp.minimum(pos + 1, width).astype(F32)
    return win / count - ua


def _conv_act(load_shifted, dw_ref, cb_ref, lg_ref, lb_ref):
    acc = None
    for i in range(CONV_W):
        term = load_shifted(CONV_BUF - i) * dw_ref[i]
        acc = term if acc is None else acc + term
    cc = _layernorm(acc + cb_ref[...], lg_ref[...], lb_ref[...])
    return cc * jax.nn.sigmoid(cc)


SUBLANES = 8
BRANCH_ROWS = 64


def _phase_rows(hist, ts):
    return ts + SUBLANES * ((hist - 1) // SUBLANES)


def _shifted_loader(ext_ref, phase_ref, hist, ts):
    rows = _phase_rows(hist, ts)
    for s in range(1, SUBLANES):
        phase_ref[s - 1] = ext_ref[pl.ds(s, rows), :]

    def load(d, row0, n_rows):
        m, s = divmod(hist - d, SUBLANES)
        src = ext_ref if s == 0 else phase_ref.at[s - 1]
        return src[pl.ds(SUBLANES * m + row0, n_rows), :]

    return load


def _branch_sample_kernel(pext_ref, cext_ref, dw_ref, cb_ref, lg_ref, lb_ref, p_ref, act_ref, *, start_pos, t_new):
    c_pool = pext_ref.shape[-1]
    shape = pext_ref.shape[1:]
    for t in range(t_new):
        pos = jnp.full(shape, start_pos + t, jnp.int32)
        p = _pool_windows(lambda d: pext_ref[POOL_BUF + t - d], pext_ref[POOL_BUF + t], pos, c_pool)
        p_ref[t] = p.astype(BF16)
        act = _conv_act(lambda d: cext_ref[CONV_BUF + t - d], dw_ref, cb_ref, lg_ref, lb_ref)
        act_ref[t] = act.astype(BF16)


def _branch_sample(pext, cext, conv_dw, conv_b, ln_g, ln_b, start_pos, t_new):
    _, db, c = pext.shape
    return pl.pallas_call(
        functools.partial(_branch_sample_kernel, start_pos=start_pos, t_new=t_new),
        out_shape=[jax.ShapeDtypeStruct((t_new, db, c), BF16), jax.ShapeDtypeStruct((t_new, db, c), BF16)],
        name="branch_sample",
    )(pext, cext, conv_dw.reshape(CONV_W, 1, c), conv_b.reshape(1, c), ln_g.reshape(1, c), ln_b.reshape(1, c))


def _suffix_matrix(tk):
    j = lax.broadcasted_iota(jnp.int32, (tk, tk), 0)
    s = lax.broadcasted_iota(jnp.int32, (tk, tk), 1)
    return jnp.where(j > s, 1.0, 0.0).astype(BF16)


def _sb_block(z, mask, u, carry):
    tk = z.shape[1]
    log_rest = jnp.minimum(-z, 0.0) - jnp.log(1.0 + jnp.exp(-jnp.abs(z)))
    if mask is not None:
        log_rest = jnp.where(mask, log_rest, 0.0)
    sums = _dot(log_rest.astype(BF16), u)
    right = jnp.concatenate([carry] * (tk // LANES), axis=1)
    a = jnp.exp(z + log_rest + sums + right)
    if mask is not None:
        a = jnp.where(mask, a, 0.0)
    return a, carry + jnp.sum(log_rest, axis=1, keepdims=True)


NEG_BIG = -1e30


def _attn_prompt_kernel(bias_ref, q_ref, k_ref, v_ref, u_ref, o_ref,
                        acc_ref, carry_ref, logsig_ref, rest_ref, total_ref, *, tq, tk):
    hp = pl.program_id(1)
    qi = pl.program_id(2)
    lane = lax.broadcasted_iota(jnp.int32, (tq, HEAD_PAIR), 1)
    first = lane < HEAD_DIM
    q = q_ref[...]
    zero = jnp.zeros_like(q)
    q_heads = (jnp.where(first, q, zero), jnp.where(first, zero, q))
    bias = (bias_ref[2 * hp] * LOG2_E, bias_ref[2 * hp + 1] * LOG2_E)
    acc_ref[...] = jnp.zeros_like(acc_ref)
    carry_ref[...] = jnp.zeros_like(carry_ref)

    def logits(j, slot, mask, row0=0):
        kb = k_ref[j]
        for h in range(2):
            z = _dot(q_heads[h][row0:], kb) + bias[h]
            neg_abs = pltpu.bitcast(pltpu.bitcast(z, jnp.uint32) | jnp.uint32(0x80000000), F32)
            softplus = jnp.maximum(z, 0.0) + jnp.log2(1.0 + jnp.exp2(neg_abs))
            log_sig = z - softplus
            if mask is not None:
                softplus = jnp.where(mask[row0:], softplus, 0.0)
                log_sig = jnp.where(mask[row0:], log_sig, NEG_BIG)
            logsig_ref[slot, h, row0:] = log_sig
            rest_ref[slot, h, row0:] = softplus.astype(BF16)
            total_ref[slot, h, row0:] = jnp.broadcast_to(jnp.sum(softplus, axis=1, keepdims=True),
                                                         (tq - row0, LANES))

    def weigh(j, slot, row0=0):
        vb = v_ref[j]
        for h in range(2):
            sums = _dot(rest_ref[slot, h, row0:], u_ref[...])
            carry = carry_ref[h, row0:]
            right = jnp.concatenate([carry] * (tk // LANES), axis=1)
            a = jnp.exp2(logsig_ref[slot, h, row0:] + sums + right)
            acc_ref[h, row0:] += lax.dot_general(a.astype(BF16), vb, (((1,), (1,)), ((), ())),
                                                 preferred_element_type=F32)
            carry_ref[h, row0:] = carry - total_ref[slot, h, row0:]

    ratio = tq // tk
    n = (qi + 1) * ratio
    row = lax.broadcasted_iota(jnp.int32, (tq, tk), 0)
    colk = lax.broadcasted_iota(jnp.int32, (tq, tk), 1)
    diag_mask = lambda i: colk + (ratio - 1 - i) * tk < row
    diag_row0 = lambda i: (ratio - 1 - i) * tk
    logits(n - 1, 0, diag_mask(0), diag_row0(0))
    for jj in range(ratio - 1):
        weigh(n - 1 - jj, jj % 2, diag_row0(jj))
        logits(n - 2 - jj, (jj + 1) % 2, diag_mask(jj + 1), diag_row0(jj + 1))

    assert ratio % 2 == 0
    first_slot = (ratio - 1) % 2

    def pair(i):
        jj = ratio - 1 + 2 * i
        weigh(n - 1 - jj, first_slot)
        logits(n - 2 - jj, 1 - first_slot, None)
        weigh(n - 2 - jj, 1 - first_slot)
        logits(n - 3 - jj, first_slot, None)

    def two_pairs(i, c):
        pair(2 * i)
        pair(2 * i + 1)
        return c

    pairs = qi * (ratio // 2)
    lax.fori_loop(0, pairs // 2, two_pairs, 0)

    @pl.when(pairs % 2 == 1)
    def _():
        pair(pairs - 1)

    weigh(0, first_slot)
    o_ref[...] = jnp.where(first, acc_ref[0], acc_ref[1]).astype(BF16)


def _attn_prompt(q, kb, vb, sb_bias, batch, seq, tq, tk):
    m, w = q.shape
    assert seq % tq == 0 and tq % tk == 0 and w % HEAD_PAIR == 0 and kb.shape == (batch, seq // tk, w, tk)
    nq = seq // tq
    qspec = pl.BlockSpec((tq, HEAD_PAIR), lambda b, h, i: (b * nq + i, h))
    kvspec = pl.BlockSpec((None, seq // tk, HEAD_PAIR, tk), lambda b, h, i: (b, 0, h, 0))
    return pl.pallas_call(
        functools.partial(_attn_prompt_kernel, tq=tq, tk=tk),
        grid=(batch, w // HEAD_PAIR, nq),
        in_specs=[pl.BlockSpec(memory_space=pltpu.SMEM), qspec, kvspec, kvspec, _const_spec((tk, tk))],
        out_specs=qspec,
        out_shape=jax.ShapeDtypeStruct((m, w), BF16),
        scratch_shapes=[pltpu.VMEM((2, tq, HEAD_PAIR), F32), pltpu.VMEM((2, tq, LANES), F32),
                        pltpu.VMEM((2, 2, tq, tk), F32), pltpu.VMEM((2, 2, tq, tk), BF16),
                        pltpu.VMEM((2, 2, tq, LANES), F32)],
        compiler_params=_cparams(("parallel", "parallel", "arbitrary")),
        name="attn_prompt",
    )(sb_bias, q, kb, vb, -_suffix_matrix(tk))


def _attn_sample_kernel(pt_ref, bias_ref, q_ref, kn_ref, vn_ref, u_ref, *rest, n_heads, t_new, pages_per_step):
    k_refs = rest[:pages_per_step]
    v_refs = rest[pages_per_step:2 * pages_per_step]
    o_ref, qrow_ref, acc_ref, carry_ref = rest[2 * pages_per_step:]
    step = pl.program_id(1)
    rows = t_new * n_heads
    width = n_heads * HEAD_DIM
    page = kn_ref.shape[1]
    own = (lax.broadcasted_iota(jnp.int32, (rows, width), 1) // HEAD_DIM
           == lax.broadcasted_iota(jnp.int32, (rows, width), 0) % n_heads)

    def block(kt, vt, mask):
        z = _dot(qrow_ref[...], kt.astype(BF16)) + bias_ref[...]
        a, carry = _sb_block(z, mask, u_ref[...], carry_ref[...])
        carry_ref[...] = carry
        acc_ref[...] += lax.dot_general(a.astype(BF16), vt.astype(BF16), (((1,), (1,)), ((), ())),
                                        preferred_element_type=F32)

    @pl.when(step == 0)
    def _():
        q = q_ref[...]
        qrep = jnp.broadcast_to(q[:, None, :], (t_new, n_heads, width)).reshape(rows, width)
        qrow_ref[...] = jnp.where(own, qrep, jnp.zeros_like(qrep))
        acc_ref[...] = jnp.zeros_like(acc_ref)
        carry_ref[...] = jnp.zeros_like(carry_ref)
        t_of_row = lax.broadcasted_iota(jnp.int32, (rows, page), 0) // n_heads
        key = lax.broadcasted_iota(jnp.int32, (rows, page), 1)
        block(kn_ref[...], vn_ref[...], key < t_of_row)

    n = pages_per_step
    stack = lambda parts: jnp.concatenate(parts, axis=0)
    kt = jnp.concatenate([r[...].astype(BF16) for r in k_refs], axis=1)
    vt = jnp.concatenate([r[...].astype(BF16) for r in v_refs], axis=1)
    z_wide = _dot(qrow_ref[...], kt)
    z = stack([z_wide[:, i * page:(i + 1) * page] for i in range(n)]) + stack([bias_ref[...]] * n)
    log_rest = jnp.minimum(-z, 0.0) - jnp.log(1.0 + jnp.exp(-jnp.abs(z)))
    sums = _dot(log_rest.astype(BF16), u_ref[...])
    totals = jnp.sum(log_rest, axis=1, keepdims=True)
    carry = carry_ref[...]
    rights = []
    for i in range(n):
        rights.append(carry)
        carry = carry + totals[i * rows:(i + 1) * rows]
    carry_ref[...] = carry
    right = jnp.concatenate([stack(rights)] * (page // LANES), axis=1)
    a = jnp.exp(z + log_rest + sums + right).astype(BF16)
    a_wide = jnp.concatenate([a[i * rows:(i + 1) * rows] for i in range(n)], axis=1)
    acc_ref[...] += lax.dot_general(a_wide, vt, (((1,), (1,)), ((), ())), preferred_element_type=F32)

    @pl.when(step == pl.num_programs(1) - 1)
    def _():
        picked = jnp.where(own, acc_ref[...], 0.0).reshape(t_new, n_heads, width)
        o_ref[...] = jnp.sum(picked, axis=1).astype(BF16)


def _attn_sample(q, k_new, v_new, cache_k, cache_v, layer, page_table, sb_bias, pages_per_step):
    db, t_new, w = q.shape
    n_heads = w // HEAD_DIM
    depth, pool_pages, page = cache_k.shape[:3]
    n_pages = page_table.shape[1]
    assert n_pages % pages_per_step == 0 and page % LANES == 0 and t_new <= page
    rows = t_new * n_heads
    paged = lambda c: c.transpose(0, 1, 3, 4, 2).reshape(depth, pool_pages, w, page)
    new_page = lambda a: jnp.pad(a.transpose(0, 2, 1), ((0, 0), (0, 0), (0, page - t_new)))
    bias_rows = jnp.tile(sb_bias, t_new).reshape(rows, 1).astype(F32)

    def page_spec(i):
        def idx(b, s, pt):
            return (layer, pt[b * n_pages + (n_pages - 1 - (s * pages_per_step + i))], 0, 0)
        return pl.BlockSpec((None, None, w, page), idx)

    per_b = lambda r, c: pl.BlockSpec((None, r, c), lambda b, s, pt: (b, 0, 0))
    const2 = lambda shape: pl.BlockSpec(shape, lambda b, s, pt: (0, 0))
    grid_spec = pltpu.PrefetchScalarGridSpec(
        num_scalar_prefetch=1,
        grid=(db, n_pages // pages_per_step),
        in_specs=[const2((rows, 1)), per_b(t_new, w), per_b(w, page), per_b(w, page), const2((page, page))]
                 + [page_spec(i) for i in range(pages_per_step)] * 2,
        out_specs=per_b(t_new, w),
        scratch_shapes=[pltpu.VMEM((rows, w), BF16), pltpu.VMEM((rows, w), F32), pltpu.VMEM((rows, LANES), F32)],
    )
    return pl.pallas_call(
        functools.partial(_attn_sample_kernel, n_heads=n_heads, t_new=t_new, pages_per_step=pages_per_step),
        grid_spec=grid_spec,
        out_shape=jax.ShapeDtypeStruct((db, t_new, w), BF16),
        compiler_params=_cparams(("parallel", "arbitrary")),
        name="attn_sample",
    )(page_table.reshape(-1), bias_rows, q, new_page(k_new), new_page(v_new), _suffix_matrix(page),
      *([paged(cache_k)] * pages_per_step), *([paged(cache_v)] * pages_per_step))


MERGE_ROWS = 512

def _merge_kernel(x_ref, p_ref, o_ref, act_ref, gate_ref, wp_ref, ps_ref, wo_ref, wc_ref, wout_ref, lg_ref, lb_ref,
                  y_ref, *, alpha):
    tm, d = x_ref.shape
    sub = min(tm, MERGE_ROWS)
    w_o, w_c, w_out = wo_ref[...].astype(BF16), wc_ref[...].astype(BF16), wout_ref[...].astype(BF16)
    for r in range(0, tm, sub):
        rows = slice(r, r + sub)
        br_a = _dot(p_ref[rows, :], wp_ref[...]) * ps_ref[...]
        br_b = _dot(o_ref[rows, :], w_o)
        br_c = _dot(act_ref[rows, :], w_c)
        merged = (gate_ref[rows, 0:d].astype(F32) * br_a + gate_ref[rows, d:2 * d].astype(F32) * br_b
                  + gate_ref[rows, 2 * d:3 * d].astype(F32) * br_c)
        y = _dot(merged.astype(BF16), w_out)
        y_ref[rows, :] = _layernorm(alpha * x_ref[rows, :] + y, lg_ref[...], lb_ref[...])


def _merge(x, p, o, act, gates, w_pool, pool_scale, w_att_o, conv_pw, w_out, ln_g, ln_b, alpha, tm):
    m, d = x.shape
    assert m % tm == 0
    row = lambda a: pl.BlockSpec((tm, a.shape[1]), lambda i: (i, 0))
    full = lambda a: _const_spec(a.shape)
    vec = lambda a: a.reshape(1, d)
    args = (x, p, o, act, gates, w_pool, vec(pool_scale), w_att_o, conv_pw, w_out, vec(ln_g), vec(ln_b))
    return pl.pallas_call(
        functools.partial(_merge_kernel, alpha=alpha),
        grid=(m // tm,),
        in_specs=[row(a) for a in args[:5]] + [full(a) for a in args[5:]],
        out_specs=pl.BlockSpec((tm, d), lambda i: (i, 0)),
        out_shape=jax.ShapeDtypeStruct((m, d), F32),
        compiler_params=_cparams(("parallel",)),
        name="merge",
    )(*args)


def _ffn_kernel(x_ref, wg_ref, wu_ref, wd_ref, lg_ref, lb_ref, y_ref, *, alpha, chunk):
    x = x_ref[...]
    xb = x.astype(BF16)
    d_ff = wg_ref.shape[1]
    acc = None
    for c in range(0, d_ff, chunk):
        n = min(chunk, d_ff - c)
        g = _dot(xb, wg_ref[:, c:c + n].astype(BF16))
        u = _dot(xb, wu_ref[:, c:c + n].astype(BF16))
        h = (g * jax.nn.sigmoid(g) * u).astype(BF16)
        part = _dot(h, wd_ref[c:c + n, :].astype(BF16))
        acc = part if acc is None else acc + part
    y_ref[...] = _layernorm(alpha * x + acc, lg_ref[...], lb_ref[...])


def _ffn(x, wg, wu, wd, ln_g, ln_b, alpha, tm):
    m, d = x.shape
    assert m % tm == 0
    vec = lambda a: a.reshape(1, d)
    return pl.pallas_call(
        functools.partial(_ffn_kernel, alpha=alpha, chunk=512),
        grid=(m // tm,),
        in_specs=[pl.BlockSpec((tm, d), lambda i: (i, 0)), _const_spec(wg.shape), _const_spec(wu.shape),
                  _const_spec(wd.shape), _const_spec((1, d)), _const_spec((1, d))],
        out_specs=pl.BlockSpec((tm, d), lambda i: (i, 0)),
        out_shape=jax.ShapeDtypeStruct((m, d), F32),
        compiler_params=_cparams(("parallel",)),
        name="ffn",
    )(x, wg, wu, wd, vec(ln_g), vec(ln_b))


def _row_tile(m, want):
    t = min(want, m)
    while m % t:
        t //= 2
    return t


def kernel(x_prompt, x_sample, cache_k, cache_v, state_pool, state_conv, page_table, w_in, b_gate, sb_bias, pool_w, pool_scale, w_att_o, conv_dw, conv_b, conv_ln_g, conv_ln_b, conv_pw, w_out, ln1_g, ln1_b, ffn_w_gate, ffn_w_up, ffn_w_down, ln2_g, ln2_b):
    depth = w_in.shape[0]
    batch, seq, d = x_prompt.shape
    db, t_new, _ = x_sample.shape
    n_pages = page_table.shape[1]
    page = cache_k.shape[2]
    past = n_pages * page
    alpha = (2.0 * depth) ** 0.25
    w_att = d // 2
    n_heads = w_att // HEAD_DIM
    c_pool = d // 4
    n_groups = len(POOL_WINDOWS)
    gw = c_pool // n_groups

    xp = x_prompt.reshape(batch * seq, d)
    xs = x_sample.reshape(db * t_new, d)
    mp, ms = xp.shape[0], xs.shape[0]
    tm_p = _row_tile(mp, 512)
    tm_s = _row_tile(ms, 128)
    tk = _row_tile(seq, 256)
    tq = _row_tile(seq, 512)

    outs = {n: [] for n in ("pp", "cp", "ks", "vs", "ps", "cs")}
    kv_all = None
    for l in range(depth):
        w_in_b = w_in[l]
        w_kvt = lax.optimization_barrier(w_in[l, :, c_pool + w_att:c_pool + 3 * w_att]).T.astype(BF16)
        w_pool = jnp.zeros((c_pool, d), F32)
        for g in range(n_groups):
            w_pool = w_pool.at[g * gw:(g + 1) * gw, g * (d // n_groups):(g + 1) * (d // n_groups)].set(pool_w[l, g])
        w_pool = w_pool.astype(BF16)
        w_o_b, w_c_b, w_out_b = w_att_o[l], conv_pw[l], w_out[l]
        wg_b, wu_b, wd_b = ffn_w_gate[l], ffn_w_up[l], ffn_w_down[l]

        def tail(x, p, o, act, gates, tm):
            x1 = _merge(x, p, o, act, gates, w_pool, pool_scale[l], w_o_b, w_c_b, w_out_b, ln1_g[l], ln1_b[l], alpha,
                        _row_tile(x.shape[0], 2 * tm))
            return _ffn(x1, wg_b, wu_b, wd_b, ln2_g[l], ln2_b[l], alpha, tm)

        conv = (conv_dw[l], conv_b[l], conv_ln_g[l], conv_ln_b[l])
        ua, q, kt, vt, ktb, vtb, glu, gates, p, act = _proj(
            xp, w_in_b, w_kvt, b_gate[l], conv, tm_p, HEAD_DIM ** -0.5 * LOG2_E,
            seq=seq, kv_block=tk, layer=l, depth=depth, kv_all=kv_all)
        kv_all = (kt, vt)
        o = _attn_prompt(q, ktb, vtb, sb_bias[l], batch, seq, tq, tk)
        xp = tail(xp, p, o, act, gates, tm_p)
        outs["pp"].append(ua.reshape(batch, seq, c_pool)[:, seq - POOL_BUF:])
        outs["cp"].append(glu.reshape(batch, seq, c_pool)[:, seq - CONV_BUF:])

        ua, q, k, v, kb, vb, glu, gates = _proj(xs, w_in_b, w_kvt, b_gate[l], conv, tm_s, HEAD_DIM ** -0.5)
        pext = jnp.concatenate([state_pool[l], ua.reshape(db, t_new, c_pool)], axis=1)
        cext = jnp.concatenate([state_conv[l], glu.reshape(db, t_new, c_pool)], axis=1)
        p, act = _branch_sample(pext.transpose(1, 0, 2), cext.transpose(1, 0, 2), conv_dw[l], conv_b[l],
                                conv_ln_g[l], conv_ln_b[l], past, t_new)
        p = p.transpose(1, 0, 2).reshape(ms, c_pool)
        act = act.transpose(1, 0, 2).reshape(ms, c_pool)
        per_b = lambda a: a.reshape(db, t_new, w_att)
        heads = lambda a: a.reshape(db, t_new, n_heads, HEAD_DIM)
        o = _attn_sample(per_b(q), per_b(kb), per_b(vb), cache_k, cache_v, l, page_table, sb_bias[l],
                         pages_per_step=_row_tile(n_pages, 32))
        xs = tail(xs, p, o.reshape(ms, w_att), act, gates, tm_s)
        outs["ks"].append(heads(k))
        outs["vs"].append(heads(v))
        outs["ps"].append(pext[:, -POOL_BUF:])
        outs["cs"].append(cext[:, -CONV_BUF:])

    st = lambda n: jnp.stack(outs[n])
    per_head = lambda a: a.reshape(depth, batch, n_heads, HEAD_DIM, seq).transpose(0, 1, 4, 2, 3)
    return (xp.reshape(batch, seq, d), xs.reshape(db, t_new, d), per_head(kv_all[0]), per_head(kv_all[1]),
            st("pp"), st("cp"),
            st("ks"), st("vs"), st("ps"), st("cs"))
```

```python
import functools

import jax
import jax.numpy as jnp
from jax import lax
from jax.experimental import pallas as pl
from jax.experimental.pallas import tpu as pltpu

HEAD_DIM = 64
HEAD_PAIR = 2 * HEAD_DIM
POOL_WINDOWS = (2, 4, 8, 16)
POOL_BUF = max(POOL_WINDOWS) - 1
POOL_HIST = 16
CONV_W = 31
CONV_BUF = CONV_W - 1
CONV_HIST = 32
N_BRANCH = 3
LN_EPS = 1e-5
LOG2_E = 1.4426950408889634
LANES = 128
VMEM_LIMIT = 56 * 1024 * 1024

F32 = jnp.float32
BF16 = jnp.bfloat16


def _cparams(sem):
    return pltpu.CompilerParams(dimension_semantics=sem, vmem_limit_bytes=VMEM_LIMIT)


def _const_spec(shape):
    nd = len(shape)
    return pl.BlockSpec(shape, lambda *_: (0,) * nd, pipeline_mode=pl.Buffered(1))


def _layer_spec(stacked, layer):
    return pl.BlockSpec((None,) + stacked.shape[1:], lambda *_: (layer, 0, 0), pipeline_mode=pl.Buffered(1))


def _layernorm(x, g, b):
    mu = jnp.mean(x, axis=-1, keepdims=True)
    xc = x - mu
    var = jnp.mean(xc * xc, axis=-1, keepdims=True)
    return xc * lax.rsqrt(var + LN_EPS) * g + b


def _dot(a, b):
    return jnp.dot(a, b, preferred_element_type=F32)


def _proj_kernel(x_ref, w_ref, wkvt_ref, bg_ref, dw_ref, cb_ref, lg_ref, lb_ref, *refs,
                 d_pool, d_att, d_conv, d_gate, chunk, kv_block, q_scale, tiles_per_seq):
    if kv_block is None:
        ua_ref, q_ref, k_ref, v_ref, kb_ref, vb_ref, glu_ref, gate_ref = refs[-8:]
    else:
        (ua_ref, q_ref, k_ref, v_ref, kb_ref, vb_ref, glu_ref, gate_ref, p_ref, act_ref,
         pool_ext, conv_ext, pool_phase, conv_phase) = refs[-14:]
        tile = pl.program_id(0) % tiles_per_seq

        @pl.when(tile == 0)
        def _():
            pool_ext[0:POOL_HIST, :] = jnp.zeros((POOL_HIST, d_pool), F32)
            conv_ext[0:CONV_HIST, :] = jnp.zeros((CONV_HIST, d_conv), F32)

    x = x_ref[...].astype(BF16)

    def seg(lo, n):
        return _dot(x, w_ref[:, lo:lo + n].astype(BF16))

    o_glu = d_pool + 3 * d_att
    ua = seg(0, d_pool)
    ua_ref[...] = ua
    glu = seg(o_glu, d_conv) * jax.nn.sigmoid(seg(o_glu + d_conv, d_conv))
    glu_ref[...] = glu
    if kv_block is not None:
        tm = ua.shape[0]
        pool_ext[POOL_HIST:, :] = ua
        conv_ext[CONV_HIST:, :] = glu
        pool_load = _shifted_loader(pool_ext, pool_phase, POOL_HIST, tm)
        conv_load = _shifted_loader(conv_ext, conv_phase, CONV_HIST, tm)
        for r0 in range(0, tm, BRANCH_ROWS):
            rows = min(BRANCH_ROWS, tm - r0)
            pos = tile * tm + r0 + lax.broadcasted_iota(jnp.int32, (rows, 1), 0)
            p = _pool_windows(lambda d: pool_load(d, r0, rows), pool_load(0, r0, rows), pos, d_pool)
            p_ref[r0:r0 + rows, :] = p.astype(BF16)
            act = _conv_act(lambda d: conv_load(d, r0, rows), dw_ref, cb_ref, lg_ref, lb_ref)
            act_ref[r0:r0 + rows, :] = act.astype(BF16)
        pool_ext[0:POOL_HIST, :] = pool_ext[tm:tm + POOL_HIST, :]
        conv_ext[0:CONV_HIST, :] = conv_ext[tm:tm + CONV_HIST, :]
    o = d_pool
    q_ref[...] = (seg(o, d_att) * q_scale).astype(BF16)
    o += d_att
    for f_ref, b_ref in ((k_ref, kb_ref), (v_ref, vb_ref)):
        if kv_block is None:
            kv = seg(o, d_att)
            f_ref[...] = kv
            b_ref[...] = kv.astype(BF16)
        else:
            kvt = lax.dot_general(wkvt_ref[o - d_pool - d_att:o - d_pool, :], x, (((1,), (1,)), ((), ())),
                                  preferred_element_type=F32)
            if len(f_ref.shape) == 3:
                for layer_slot in range(f_ref.shape[0]):
                    f_ref[layer_slot] = kvt
            else:
                f_ref[...] = kvt
            for j in range(kvt.shape[1] // kv_block):
                b_ref[j] = kvt[:, j * kv_block:(j + 1) * kv_block].astype(BF16)
        o += d_att
    o += 2 * d_conv
    for c in range(0, d_gate, chunk):
        n = min(chunk, d_gate - c)
        gate_ref[:, c:c + n] = jax.nn.sigmoid(seg(o + c, n) + bg_ref[:, c:c + n]).astype(BF16)


def _proj(x, w_in, w_kvt, b_gate, conv, tm, q_scale, seq=None, kv_block=None, layer=0, depth=1, kv_all=None):
    m, d = x.shape
    n_in = w_in.shape[2]
    d_pool = d // 4
    d_att = d // 2
    d_conv = d // 4
    d_gate = N_BRANCH * d
    assert n_in == d_pool + 3 * d_att + 2 * d_conv + d_gate and m % tm == 0
    row = lambda n: pl.BlockSpec((tm, n), lambda i: (i, 0))
    extra_specs, extra_shapes, scratch, nt = [], [], [], 1
    if kv_block is None:
        kv_specs = [row(d_att)] * 4
        kv_shapes = [jax.ShapeDtypeStruct((m, d_att), F32)] * 2 + [jax.ShapeDtypeStruct((m, d_att), BF16)] * 2
    else:
        assert seq % tm == 0 and tm % kv_block == 0 and tm % CONV_HIST == 0
        nt = seq // tm
        extra_specs = [row(d_pool), row(d_conv)]
        extra_shapes = [jax.ShapeDtypeStruct((m, d_pool), BF16), jax.ShapeDtypeStruct((m, d_conv), BF16)]
        scratch = [pltpu.VMEM((POOL_HIST + tm, d_pool), F32), pltpu.VMEM((CONV_HIST + tm, d_conv), F32),
                   pltpu.VMEM((SUBLANES - 1, _phase_rows(POOL_HIST, tm), d_pool), F32),
                   pltpu.VMEM((SUBLANES - 1, _phase_rows(CONV_HIST, tm), d_conv), F32)]
        if layer == 0:
            f_spec = pl.BlockSpec((depth, None, d_att, tm), lambda i: (0, i // nt, 0, i % nt))
        else:
            f_spec = pl.BlockSpec((None, None, d_att, tm), lambda i: (layer, i // nt, 0, i % nt))
        b_spec = pl.BlockSpec((None, tm // kv_block, d_att, kv_block), lambda i: (i // nt, i % nt, 0, 0))
        kv_specs = [f_spec, f_spec, b_spec, b_spec]
        kv_shapes = ([jax.ShapeDtypeStruct((depth, m // seq, d_att, seq), F32)] * 2
                     + [jax.ShapeDtypeStruct((m // seq, seq // kv_block, d_att, kv_block), BF16)] * 2)
    kern = functools.partial(_proj_kernel, d_pool=d_pool, d_att=d_att, d_conv=d_conv, d_gate=d_gate, chunk=512,
                             kv_block=kv_block, q_scale=q_scale, tiles_per_seq=nt)
    conv_dw, conv_b, ln_g, ln_b = conv
    vec = _const_spec((1, d_conv))
    in_specs = [row(d), _layer_spec(w_in, layer), _const_spec(w_kvt.shape), _const_spec((1, d_gate)),
                _const_spec((CONV_W, 1, d_conv)), vec, vec, vec]
    args = [x, w_in, w_kvt, b_gate.reshape(1, d_gate), conv_dw.reshape(CONV_W, 1, d_conv),
            conv_b.reshape(1, d_conv), ln_g.reshape(1, d_conv), ln_b.reshape(1, d_conv)]
    aliases = {}
    if kv_all is not None:
        aliases = {len(args): 2, len(args) + 1: 3}
        in_specs += [pl.BlockSpec(memory_space=pl.ANY)] * 2
        args += list(kv_all)
    return pl.pallas_call(
        kern,
        grid=(m // tm,),
        in_specs=in_specs,
        out_specs=[row(d_pool), row(d_att)] + kv_specs + [row(d_conv), row(d_gate)] + extra_specs,
        out_shape=[jax.ShapeDtypeStruct((m, d_pool), F32), jax.ShapeDtypeStruct((m, d_att), BF16)] + kv_shapes
                  + [jax.ShapeDtypeStruct((m, d_conv), F32), jax.ShapeDtypeStruct((m, d_gate), BF16)] + extra_shapes,
        scratch_shapes=scratch,
        input_output_aliases=aliases,
        compiler_params=_cparams(("arbitrary",)),
        name="proj",
    )(*args)


def _pool_windows(load_shifted, ua, pos, c_pool):
    gw = c_pool // len(POOL_WINDOWS)
    col = lax.broadcasted_iota(jnp.int32, (1, c_pool), 1)
    run = ua
    win = None
    width = jnp.zeros((1, c_pool), jnp.int32)
    for d in range(1, max(POOL_WINDOWS)):
        run = run + load_shifted(d)
        if d + 1 in POOL_WINDOWS:
            g = POOL_WINDOWS.index(d + 1)
            in_group = (col >= g * gw) & (col < (g + 1) * gw)
            win = run if win is None else jnp.where(in_group, run, win)
            width = jnp.where(in_group, d + 1, width)
    count = jnp.minimum(pos + 1, width).astype(F32)
    return win / count - ua


def _conv_act(load_shifted, dw_ref, cb_ref, lg_ref, lb_ref):
    acc = None
    for i in range(CONV_W):
        term = load_shifted(CONV_BUF - i) * dw_ref[i]
        acc = term if acc is None else acc + term
    cc = _layernorm(acc + cb_ref[...], lg_ref[...], lb_ref[...])
    return cc * jax.nn.sigmoid(cc)


SUBLANES = 8
BRANCH_ROWS = 64


def _phase_rows(hist, ts):
    return ts + SUBLANES * ((hist - 1) // SUBLANES)


def _shifted_loader(ext_ref, phase_ref, hist, ts):
    rows = _phase_rows(hist, ts)
    for s in range(1, SUBLANES):
        phase_ref[s - 1] = ext_ref[pl.ds(s, rows), :]

    def load(d, row0, n_rows):
        m, s = divmod(hist - d, SUBLANES)
        src = ext_ref if s == 0 else phase_ref.at[s - 1]
        return src[pl.ds(SUBLANES * m + row0, n_rows), :]

    return load


def _branch_sample_kernel(pext_ref, cext_ref, dw_ref, cb_ref, lg_ref, lb_ref, p_ref, act_ref, *, start_pos, t_new):
    c_pool = pext_ref.shape[-1]
    shape = pext_ref.shape[1:]
    for t in range(t_new):
        pos = jnp.full(shape, start_pos + t, jnp.int32)
        p = _pool_windows(lambda d: pext_ref[POOL_BUF + t - d], pext_ref[POOL_BUF + t], pos, c_pool)
        p_ref[t] = p.astype(BF16)
        act = _conv_act(lambda d: cext_ref[CONV_BUF + t - d], dw_ref, cb_ref, lg_ref, lb_ref)
        act_ref[t] = act.astype(BF16)


def _branch_sample(pext, cext, conv_dw, conv_b, ln_g, ln_b, start_pos, t_new):
    _, db, c = pext.shape
    return pl.pallas_call(
        functools.partial(_branch_sample_kernel, start_pos=start_pos, t_new=t_new),
        out_shape=[jax.ShapeDtypeStruct((t_new, db, c), BF16), jax.ShapeDtypeStruct((t_new, db, c), BF16)],
        name="branch_sample",
    )(pext, cext, conv_dw.reshape(CONV_W, 1, c), conv_b.reshape(1, c), ln_g.reshape(1, c), ln_b.reshape(1, c))


def _suffix_matrix(tk):
    j = lax.broadcasted_iota(jnp.int32, (tk, tk), 0)
    s = lax.broadcasted_iota(jnp.int32, (tk, tk), 1)
    return jnp.where(j > s, 1.0, 0.0).astype(BF16)


def _sb_block(z, mask, u, carry):
    tk = z.shape[1]
    log_rest = jnp.minimum(-z, 0.0) - jnp.log(1.0 + jnp.exp(-jnp.abs(z)))
    if mask is not None:
        log_rest = jnp.where(mask, log_rest, 0.0)
    sums = _dot(log_rest.astype(BF16), u)
    right = jnp.concatenate([carry] * (tk // LANES), axis=1)
    a = jnp.exp(z + log_rest + sums + right)
    if mask is not None:
        a = jnp.where(mask, a, 0.0)
    return a, carry + jnp.sum(log_rest, axis=1, keepdims=True)


NEG_BIG = -1e30


def _attn_prompt_kernel(bias_ref, q_ref, k_ref, v_ref, u_ref, o_ref,
                        acc_ref, carry_ref, logsig_ref, rest_ref, total_ref, *, tq, tk):
    hp = pl.program_id(1)
    qi = pl.program_id(2)
    lane = lax.broadcasted_iota(jnp.int32, (tq, HEAD_PAIR), 1)
    first = lane < HEAD_DIM
    q = q_ref[...]
    zero = jnp.zeros_like(q)
    q_heads = (jnp.where(first, q, zero), jnp.where(first, zero, q))
    bias = (bias_ref[2 * hp] * LOG2_E, bias_ref[2 * hp + 1] * LOG2_E)
    acc_ref[...] = jnp.zeros_like(acc_ref)
    carry_ref[...] = jnp.zeros_like(carry_ref)

    def logits(j, slot, mask, row0=0):
        kb = k_ref[j]
        for h in range(2):
            z = _dot(q_heads[h][row0:], kb) + bias[h]
            neg_abs = pltpu.bitcast(pltpu.bitcast(z, jnp.uint32) | jnp.uint32(0x80000000), F32)
            softplus = jnp.maximum(z, 0.0) + jnp.log2(1.0 + jnp.exp2(neg_abs))
            log_sig = z - softplus
            if mask is not None:
                softplus = jnp.where(mask[row0:], softplus, 0.0)
                log_sig = jnp.where(mask[row0:], log_sig, NEG_BIG)
            logsig_ref[slot, h, row0:] = log_sig
            rest_ref[slot, h, row0:] = softplus.astype(BF16)
            total_ref[slot, h, row0:] = jnp.broadcast_to(jnp.sum(softplus, axis=1, keepdims=True),
                                                         (tq - row0, LANES))

    def weigh(j, slot, row0=0):
        vb = v_ref[j]
        for h in range(2):
            sums = _dot(rest_ref[slot, h, row0:], u_ref[...])
            carry = carry_ref[h, row0:]
            right = jnp.concatenate([carry] * (tk // LANES), axis=1)
            a = jnp.exp2(logsig_ref[slot, h, row0:] + sums + right)
            acc_ref[h, row0:] += lax.dot_general(a.astype(BF16), vb, (((1,), (1,)), ((), ())),
                                                 preferred_element_type=F32)
            carry_ref[h, row0:] = carry - total_ref[slot, h, row0:]

    ratio = tq // tk
    n = (qi + 1) * ratio
    row = lax.broadcasted_iota(jnp.int32, (tq, tk), 0)
    colk = lax.broadcasted_iota(jnp.int32, (tq, tk), 1)
    diag_mask = lambda i: colk + (ratio - 1 - i) * tk < row
    diag_row0 = lambda i: (ratio - 1 - i) * tk
    logits(n - 1, 0, diag_mask(0), diag_row0(0))
    for jj in range(ratio - 1):
        weigh(n - 1 - jj, jj % 2, diag_row0(jj))
        logits(n - 2 - jj, (jj + 1) % 2, diag_mask(jj + 1), diag_row0(jj + 1))

    assert ratio % 2 == 0
    first_slot = (ratio - 1) % 2

    def pair(i):
        jj = ratio - 1 + 2 * i
        weigh(n - 1 - jj, first_slot)
        logits(n - 2 - jj, 1 - first_slot, None)
        weigh(n - 2 - jj, 1 - first_slot)
        logits(n - 3 - jj, first_slot, None)

    def two_pairs(i, c):
        pair(2 * i)
        pair(2 * i + 1)
        return c

    pairs = qi * (ratio // 2)
    lax.fori_loop(0, pairs // 2, two_pairs, 0)

    @pl.when(pairs % 2 == 1)
    def _():
        pair(pairs - 1)

    weigh(0, first_slot)
    o_ref[...] = jnp.where(first, acc_ref[0], acc_ref[1]).astype(BF16)


def _attn_prompt(q, kb, vb, sb_bias, batch, seq, tq, tk):
    m, w = q.shape
    assert seq % tq == 0 and tq % tk == 0 and w % HEAD_PAIR == 0 and kb.shape == (batch, seq // tk, w, tk)
    nq = seq // tq
    qspec = pl.BlockSpec((tq, HEAD_PAIR), lambda b, h, i: (b * nq + i, h))
    kvspec = pl.BlockSpec((None, seq // tk, HEAD_PAIR, tk), lambda b, h, i: (b, 0, h, 0))
    return pl.pallas_call(
        functools.partial(_attn_prompt_kernel, tq=tq, tk=tk),
        grid=(batch, w // HEAD_PAIR, nq),
        in_specs=[pl.BlockSpec(memory_space=pltpu.SMEM), qspec, kvspec, kvspec, _const_spec((tk, tk))],
        out_specs=qspec,
        out_shape=jax.ShapeDtypeStruct((m, w), BF16),
        scratch_shapes=[pltpu.VMEM((2, tq, HEAD_PAIR), F32), pltpu.VMEM((2, tq, LANES), F32),
                        pltpu.VMEM((2, 2, tq, tk), F32), pltpu.VMEM((2, 2, tq, tk), BF16),
                        pltpu.VMEM((2, 2, tq, LANES), F32)],
        compiler_params=_cparams(("parallel", "parallel", "arbitrary")),
        name="attn_prompt",
    )(sb_bias, q, kb, vb, -_suffix_matrix(tk))


def _attn_sample_kernel(pt_ref, bias_ref, q_ref, kn_ref, vn_ref, u_ref, *rest, n_heads, t_new, pages_per_step):
    k_refs = rest[:pages_per_step]
    v_refs = rest[pages_per_step:2 * pages_per_step]
    o_ref, qrow_ref, acc_ref, carry_ref = rest[2 * pages_per_step:]
    step = pl.program_id(1)
    rows = t_new * n_heads
    width = n_heads * HEAD_DIM
    page = kn_ref.shape[1]
    own = (lax.broadcasted_iota(jnp.int32, (rows, width), 1) // HEAD_DIM
           == lax.broadcasted_iota(jnp.int32, (rows, width), 0) % n_heads)

    def block(kt, vt, mask):
        z = _dot(qrow_ref[...], kt.astype(BF16)) + bias_ref[...]
        a, carry = _sb_block(z, mask, u_ref[...], carry_ref[...])
        carry_ref[...] = carry
        acc_ref[...] += lax.dot_general(a.astype(BF16), vt.astype(BF16), (((1,), (1,)), ((), ())),
                                        preferred_element_type=F32)

    @pl.when(step == 0)
    def _():
        q = q_ref[...]
        qrep = jnp.broadcast_to(q[:, None, :], (t_new, n_heads, width)).reshape(rows, width)
        qrow_ref[...] = jnp.where(own, qrep, jnp.zeros_like(qrep))
        acc_ref[...] = jnp.zeros_like(acc_ref)
        carry_ref[...] = jnp.zeros_like(carry_ref)
        t_of_row = lax.broadcasted_iota(jnp.int32, (rows, page), 0) // n_heads
        key = lax.broadcasted_iota(jnp.int32, (rows, page), 1)
        block(kn_ref[...], vn_ref[...], key < t_of_row)

    n = pages_per_step
    stack = lambda parts: jnp.concatenate(parts, axis=0)
    kt = jnp.concatenate([r[...].astype(BF16) for r in k_refs], axis=1)
    vt = jnp.concatenate([r[...].astype(BF16) for r in v_refs], axis=1)
    z_wide = _dot(qrow_ref[...], kt)
    z = stack([z_wide[:, i * page:(i + 1) * page] for i in range(n)]) + stack([bias_ref[...]] * n)
    log_rest = jnp.minimum(-z, 0.0) - jnp.log(1.0 + jnp.exp(-jnp.abs(z)))
    sums = _dot(log_rest.astype(BF16), u_ref[...])
    totals = jnp.sum(log_rest, axis=1, keepdims=True)
    carry = carry_ref[...]
    rights = []
    for i in range(n):
        rights.append(carry)
        carry = carry + totals[i * rows:(i + 1) * rows]
    carry_ref[...] = carry
    right = jnp.concatenate([stack(rights)] * (page // LANES), axis=1)
    a = jnp.exp(z + log_rest + sums + right).astype(BF16)
    a_wide = jnp.concatenate([a[i * rows:(i + 1) * rows] for i in range(n)], axis=1)
    acc_ref[...] += lax.dot_general(a_wide, vt, (((1,), (1,)), ((), ())), preferred_element_type=F32)

    @pl.when(step == pl.num_programs(1) - 1)
    def _():
        picked = jnp.where(own, acc_ref[...], 0.0).reshape(t_new, n_heads, width)
        o_ref[...] = jnp.sum(picked, axis=1).astype(BF16)


def _attn_sample(q, k_new, v_new, cache_k, cache_v, layer, page_table, sb_bias, pages_per_step):
    db, t_new, w = q.shape
    n_heads = w // HEAD_DIM
    depth, pool_pages, page = cache_k.shape[:3]
    n_pages = page_table.shape[1]
    assert n_pages % pages_per_step == 0 and page % LANES == 0 and t_new <= page
    rows = t_new * n_heads
    paged = lambda c: c.transpose(0, 1, 3, 4, 2).reshape(depth, pool_pages, w, page)
    new_page = lambda a: jnp.pad(a.transpose(0, 2, 1), ((0, 0), (0, 0), (0, page - t_new)))
    bias_rows = jnp.tile(sb_bias, t_new).reshape(rows, 1).astype(F32)

    def page_spec(i):
        def idx(b, s, pt):
            return (layer, pt[b * n_pages + (n_pages - 1 - (s * pages_per_step + i))], 0, 0)
        return pl.BlockSpec((None, None, w, page), idx)

    per_b = lambda r, c: pl.BlockSpec((None, r, c), lambda b, s, pt: (b, 0, 0))
    const2 = lambda shape: pl.BlockSpec(shape, lambda b, s, pt: (0, 0))
    grid_spec = pltpu.PrefetchScalarGridSpec(
        num_scalar_prefetch=1,
        grid=(db, n_pages // pages_per_step),
        in_specs=[const2((rows, 1)), per_b(t_new, w), per_b(w, page), per_b(w, page), const2((page, page))]
                 + [page_spec(i) for i in range(pages_per_step)] * 2,
        out_specs=per_b(t_new, w),
        scratch_shapes=[pltpu.VMEM((rows, w), BF16), pltpu.VMEM((rows, w), F32), pltpu.VMEM((rows, LANES), F32)],
    )
    return pl.pallas_call(
        functools.partial(_attn_sample_kernel, n_heads=n_heads, t_new=t_new, pages_per_step=pages_per_step),
        grid_spec=grid_spec,
        out_shape=jax.ShapeDtypeStruct((db, t_new, w), BF16),
        compiler_params=_cparams(("parallel", "arbitrary")),
        name="attn_sample",
    )(page_table.reshape(-1), bias_rows, q, new_page(k_new), new_page(v_new), _suffix_matrix(page),
      *([paged(cache_k)] * pages_per_step), *([paged(cache_v)] * pages_per_step))


MERGE_ROWS = 512

def _merge_kernel(x_ref, p_ref, o_ref, act_ref, gate_ref, wp_ref, ps_ref, wo_ref, wc_ref, wout_ref, lg_ref, lb_ref,
                  y_ref, *, alpha):
    tm, d = x_ref.shape
    sub = min(tm, MERGE_ROWS)
    w_o, w_c, w_out = wo_ref[...].astype(BF16), wc_ref[...].astype(BF16), wout_ref[...].astype(BF16)
    for r in range(0, tm, sub):
        rows = slice(r, r + sub)
        br_a = _dot(p_ref[rows, :], wp_ref[...]) * ps_ref[...]
        br_b = _dot(o_ref[rows, :], w_o)
        br_c = _dot(act_ref[rows, :], w_c)
        merged = (gate_ref[rows, 0:d].astype(F32) * br_a + gate_ref[rows, d:2 * d].astype(F32) * br_b
                  + gate_ref[rows, 2 * d:3 * d].astype(F32) * br_c)
        y = _dot(merged.astype(BF16), w_out)
        y_ref[rows, :] = _layernorm(alpha * x_ref[rows, :] + y, lg_ref[...], lb_ref[...])


def _merge(x, p, o, act, gates, w_pool, pool_scale, w_att_o, conv_pw, w_out, ln_g, ln_b, alpha, tm, layer):
    m, d = x.shape
    assert m % tm == 0
    row = lambda a: pl.BlockSpec((tm, a.shape[1]), lambda i: (i, 0))
    full = lambda a: _layer_spec(a, layer) if a.ndim == 3 else _const_spec(a.shape)
    vec = lambda a: a.reshape(1, d)
    args = (x, p, o, act, gates, w_pool, vec(pool_scale), w_att_o, conv_pw, w_out, vec(ln_g), vec(ln_b))
    return pl.pallas_call(
        functools.partial(_merge_kernel, alpha=alpha),
        grid=(m // tm,),
        in_specs=[row(a) for a in args[:5]] + [full(a) for a in args[5:]],
        out_specs=pl.BlockSpec((tm, d), lambda i: (i, 0)),
        out_shape=jax.ShapeDtypeStruct((m, d), F32),
        compiler_params=_cparams(("parallel",)),
        name="merge",
    )(*args)


def _ffn_kernel(x_ref, wg_ref, wu_ref, wd_ref, lg_ref, lb_ref, y_ref, *, alpha, chunk):
    x = x_ref[...]
    xb = x.astype(BF16)
    d_ff = wg_ref.shape[1]
    acc = None
    for c in range(0, d_ff, chunk):
        n = min(chunk, d_ff - c)
        g = _dot(xb, wg_ref[:, c:c + n].astype(BF16))
        u = _dot(xb, wu_ref[:, c:c + n].astype(BF16))
        h = (g * jax.nn.sigmoid(g) * u).astype(BF16)
        part = _dot(h, wd_ref[c:c + n, :].astype(BF16))
        acc = part if acc is None else acc + part
    y_ref[...] = _layernorm(alpha * x + acc, lg_ref[...], lb_ref[...])


def _ffn(x, wg, wu, wd, ln_g, ln_b, alpha, tm, layer):
    m, d = x.shape
    assert m % tm == 0
    vec = lambda a: a.reshape(1, d)
    return pl.pallas_call(
        functools.partial(_ffn_kernel, alpha=alpha, chunk=512),
        grid=(m // tm,),
        in_specs=[pl.BlockSpec((tm, d), lambda i: (i, 0)), _layer_spec(wg, layer), _layer_spec(wu, layer),
                  _layer_spec(wd, layer), _const_spec((1, d)), _const_spec((1, d))],
        out_specs=pl.BlockSpec((tm, d), lambda i: (i, 0)),
        out_shape=jax.ShapeDtypeStruct((m, d), F32),
        compiler_params=_cparams(("parallel",)),
        name="ffn",
    )(x, wg, wu, wd, vec(ln_g), vec(ln_b))


def _row_tile(m, want):
    t = min(want, m)
    while m % t:
        t //= 2
    return t


def kernel(x_prompt, x_sample, cache_k, cache_v, state_pool, state_conv, page_table, w_in, b_gate, sb_bias, pool_w, pool_scale, w_att_o, conv_dw, conv_b, conv_ln_g, conv_ln_b, conv_pw, w_out, ln1_g, ln1_b, ffn_w_gate, ffn_w_up, ffn_w_down, ln2_g, ln2_b):
    depth = w_in.shape[0]
    batch, seq, d = x_prompt.shape
    db, t_new, _ = x_sample.shape
    n_pages = page_table.shape[1]
    page = cache_k.shape[2]
    past = n_pages * page
    alpha = (2.0 * depth) ** 0.25
    w_att = d // 2
    n_heads = w_att // HEAD_DIM
    c_pool = d // 4
    n_groups = len(POOL_WINDOWS)
    gw = c_pool // n_groups

    xp = x_prompt.reshape(batch * seq, d)
    xs = x_sample.reshape(db * t_new, d)
    mp, ms = xp.shape[0], xs.shape[0]
    tm_p = _row_tile(mp, 512)
    tm_s = _row_tile(ms, 128)
    tk = _row_tile(seq, 256)
    tq = _row_tile(seq, 512)

    outs = {n: [] for n in ("pp", "cp", "ks", "vs", "ps", "cs")}
    kv_all = None
    for l in range(depth):
        w_kvt = lax.optimization_barrier(w_in[l, :, c_pool + w_att:c_pool + 3 * w_att]).T.astype(BF16)
        w_pool = jnp.zeros((c_pool, d), F32)
        for g in range(n_groups):
            w_pool = w_pool.at[g * gw:(g + 1) * gw, g * (d // n_groups):(g + 1) * (d // n_groups)].set(pool_w[l, g])
        w_pool = w_pool.astype(BF16)

        def tail(x, p, o, act, gates, tm):
            x1 = _merge(x, p, o, act, gates, w_pool, pool_scale[l], w_att_o, conv_pw, w_out, ln1_g[l], ln1_b[l], alpha,
                        _row_tile(x.shape[0], 2 * tm), l)
            return _ffn(x1, ffn_w_gate, ffn_w_up, ffn_w_down, ln2_g[l], ln2_b[l], alpha, tm, l)

        conv = (conv_dw[l], conv_b[l], conv_ln_g[l], conv_ln_b[l])
        ua, q, kt, vt, ktb, vtb, glu, gates, p, act = _proj(
            xp, w_in, w_kvt, b_gate[l], conv, tm_p, HEAD_DIM ** -0.5 * LOG2_E,
            seq=seq, kv_block=tk, layer=l, depth=depth, kv_all=kv_all)
        kv_all = (kt, vt)
        o = _attn_prompt(q, ktb, vtb, sb_bias[l], batch, seq, tq, tk)
        xp = tail(xp, p, o, act, gates, tm_p)
        outs["pp"].append(ua.reshape(batch, seq, c_pool)[:, seq - POOL_BUF:])
        outs["cp"].append(glu.reshape(batch, seq, c_pool)[:, seq - CONV_BUF:])

        ua, q, k, v, kb, vb, glu, gates = _proj(xs, w_in, w_kvt, b_gate[l], conv, tm_s, HEAD_DIM ** -0.5, layer=l)
        pext = jnp.concatenate([state_pool[l], ua.reshape(db, t_new, c_pool)], axis=1)
        cext = jnp.concatenate([state_conv[l], glu.reshape(db, t_new, c_pool)], axis=1)
        p, act = _branch_sample(pext.transpose(1, 0, 2), cext.transpose(1, 0, 2), conv_dw[l], conv_b[l],
                                conv_ln_g[l], conv_ln_b[l], past, t_new)
        p = p.transpose(1, 0, 2).reshape(ms, c_pool)
        act = act.transpose(1, 0, 2).reshape(ms, c_pool)
        per_b = lambda a: a.reshape(db, t_new, w_att)
        heads = lambda a: a.reshape(db, t_new, n_heads, HEAD_DIM)
        o = _attn_sample(per_b(q), per_b(kb), per_b(vb), cache_k, cache_v, l, page_table, sb_bias[l],
                         pages_per_step=_row_tile(n_pages, 32))
        xs = tail(xs, p, o.reshape(ms, w_att), act, gates, tm_s)
        outs["ks"].append(heads(k))
        outs["vs"].append(heads(v))
        outs["ps"].append(pext[:, -POOL_BUF:])
        outs["cs"].append(cext[:, -CONV_BUF:])

    st = lambda n: jnp.stack(outs[n])
    per_head = lambda a: a.reshape(depth, batch, n_heads, HEAD_DIM, seq).transpose(0, 1, 4, 2, 3)
    return (xp.reshape(batch, seq, d), xs.reshape(db, t_new, d), per_head(kv_all[0]), per_head(kv_all[1]),
            st("pp"), st("cp"),
            st("ks"), st("vs"), st("ps"), st("cs"))
```

```python
import functools

import jax
import jax.numpy as jnp
from jax import lax
from jax.experimental import pallas as pl
from jax.experimental.pallas import tpu as pltpu

HEAD_DIM = 64
HEAD_PAIR = 2 * HEAD_DIM
POOL_WINDOWS = (2, 4, 8, 16)
POOL_BUF = max(POOL_WINDOWS) - 1
POOL_HIST = 16
CONV_W = 31
CONV_BUF = CONV_W - 1
CONV_HIST = 32
N_BRANCH = 3
LN_EPS = 1e-5
LOG2_E = 1.4426950408889634
LANES = 128
VMEM_LIMIT = 56 * 1024 * 1024

F32 = jnp.float32
BF16 = jnp.bfloat16


def _cparams(sem):
    return pltpu.CompilerParams(dimension_semantics=sem, vmem_limit_bytes=VMEM_LIMIT)


def _const_spec(shape):
    nd = len(shape)
    return pl.BlockSpec(shape, lambda *_: (0,) * nd, pipeline_mode=pl.Buffered(1))


def _layer_spec(stacked, layer):
    return pl.BlockSpec((None,) + stacked.shape[1:], lambda *_: (layer, 0, 0), pipeline_mode=pl.Buffered(1))


def _layernorm(x, g, b):
    mu = jnp.mean(x, axis=-1, keepdims=True)
    xc = x - mu
    var = jnp.mean(xc * xc, axis=-1, keepdims=True)
    return xc * lax.rsqrt(var + LN_EPS) * g + b


def _dot(a, b):
    return jnp.dot(a, b, preferred_element_type=F32)


def _proj_kernel(x_ref, w_ref, wkvt_ref, bg_ref, dw_ref, cb_ref, lg_ref, lb_ref, *refs,
                 d_pool, d_att, d_conv, d_gate, chunk, kv_block, q_scale, tiles_per_seq):
    if kv_block is None:
        ua_ref, q_ref, k_ref, v_ref, kb_ref, vb_ref, glu_ref, gate_ref = refs[-8:]
    else:
        (ua_ref, q_ref, k_ref, v_ref, kb_ref, vb_ref, glu_ref, gate_ref, p_ref, act_ref,
         pool_ext, conv_ext, pool_phase, conv_phase) = refs[-14:]
        tile = pl.program_id(0) % tiles_per_seq

        @pl.when(tile == 0)
        def _():
            pool_ext[0:POOL_HIST, :] = jnp.zeros((POOL_HIST, d_pool), F32)
            conv_ext[0:CONV_HIST, :] = jnp.zeros((CONV_HIST, d_conv), F32)

    x = x_ref[...].astype(BF16)

    def seg(lo, n):
        return _dot(x, w_ref[:, lo:lo + n].astype(BF16))

    o_glu = d_pool + 3 * d_att
    ua = seg(0, d_pool)
    ua_ref[...] = ua
    glu = seg(o_glu, d_conv) * jax.nn.sigmoid(seg(o_glu + d_conv, d_conv))
    glu_ref[...] = glu
    if kv_block is not None:
        tm = ua.shape[0]
        pool_ext[POOL_HIST:, :] = ua
        conv_ext[CONV_HIST:, :] = glu
        pool_load = _shifted_loader(pool_ext, pool_phase, POOL_HIST, tm)
        conv_load = _shifted_loader(conv_ext, conv_phase, CONV_HIST, tm)
        for r0 in range(0, tm, BRANCH_ROWS):
            rows = min(BRANCH_ROWS, tm - r0)
            pos = tile * tm + r0 + lax.broadcasted_iota(jnp.int32, (rows, 1), 0)
            p = _pool_windows(lambda d: pool_load(d, r0, rows), pool_load(0, r0, rows), pos, d_pool)
            p_ref[r0:r0 + rows, :] = p.astype(BF16)
            act = _conv_act(lambda d: conv_load(d, r0, rows), dw_ref, cb_ref, lg_ref, lb_ref)
            act_ref[r0:r0 + rows, :] = act.astype(BF16)
        pool_ext[0:POOL_HIST, :] = pool_ext[tm:tm + POOL_HIST, :]
        conv_ext[0:CONV_HIST, :] = conv_ext[tm:tm + CONV_HIST, :]
    o = d_pool
    q_ref[...] = (seg(o, d_att) * q_scale).astype(BF16)
    o += d_att
    for f_ref, b_ref in ((k_ref, kb_ref), (v_ref, vb_ref)):
        if kv_block is None:
            kv = seg(o, d_att)
            f_ref[...] = kv
            b_ref[...] = kv.astype(BF16)
        else:
            kvt = lax.dot_general(wkvt_ref[o - d_pool - d_att:o - d_pool, :], x, (((1,), (1,)), ((), ())),
                                  preferred_element_type=F32)
            if len(f_ref.shape) == 3:
                for layer_slot in range(f_ref.shape[0]):
                    f_ref[layer_slot] = kvt
            else:
                f_ref[...] = kvt
            for j in range(kvt.shape[1] // kv_block):
                b_ref[j] = kvt[:, j * kv_block:(j + 1) * kv_block].astype(BF16)
        o += d_att
    o += 2 * d_conv
    for c in range(0, d_gate, chunk):
        n = min(chunk, d_gate - c)
        gate_ref[:, c:c + n] = jax.nn.sigmoid(seg(o + c, n) + bg_ref[:, c:c + n]).astype(BF16)


def _proj(x, w_in, w_kvt, b_gate, conv, tm, q_scale, seq=None, kv_block=None, layer=0, depth=1, kv_all=None):
    m, d = x.shape
    n_in = w_in.shape[2]
    d_pool = d // 4
    d_att = d // 2
    d_conv = d // 4
    d_gate = N_BRANCH * d
    assert n_in == d_pool + 3 * d_att + 2 * d_conv + d_gate and m % tm == 0
    row = lambda n: pl.BlockSpec((tm, n), lambda i: (i, 0))
    extra_specs, extra_shapes, scratch, nt = [], [], [], 1
    if kv_block is None:
        kv_specs = [row(d_att)] * 4
        kv_shapes = [jax.ShapeDtypeStruct((m, d_att), F32)] * 2 + [jax.ShapeDtypeStruct((m, d_att), BF16)] * 2
    else:
        assert seq % tm == 0 and tm % kv_block == 0 and tm % CONV_HIST == 0
        nt = seq // tm
        extra_specs = [row(d_pool), row(d_conv)]
        extra_shapes = [jax.ShapeDtypeStruct((m, d_pool), BF16), jax.ShapeDtypeStruct((m, d_conv), BF16)]
        scratch = [pltpu.VMEM((POOL_HIST + tm, d_pool), F32), pltpu.VMEM((CONV_HIST + tm, d_conv), F32),
                   pltpu.VMEM((SUBLANES - 1, _phase_rows(POOL_HIST, tm), d_pool), F32),
                   pltpu.VMEM((SUBLANES - 1, _phase_rows(CONV_HIST, tm), d_conv), F32)]
        if layer == 0:
            f_spec = pl.BlockSpec((depth, None, d_att, tm), lambda i: (0, i // nt, 0, i % nt))
        else:
            f_spec = pl.BlockSpec((None, None, d_att, tm), lambda i: (layer, i // nt, 0, i % nt))
        b_spec = pl.BlockSpec((None, tm // kv_block, d_att, kv_block), lambda i: (i // nt, i % nt, 0, 0))
        kv_specs = [f_spec, f_spec, b_spec, b_spec]
        kv_shapes = ([jax.ShapeDtypeStruct((depth, m // seq, d_att, seq), F32)] * 2
                     + [jax.ShapeDtypeStruct((m // seq, seq // kv_block, d_att, kv_block), BF16)] * 2)
    kern = functools.partial(_proj_kernel, d_pool=d_pool, d_att=d_att, d_conv=d_conv, d_gate=d_gate, chunk=512,
                             kv_block=kv_block, q_scale=q_scale, tiles_per_seq=nt)
    conv_dw, conv_b, ln_g, ln_b = conv
    vec = _const_spec((1, d_conv))
    in_specs = [row(d), _layer_spec(w_in, layer), _const_spec(w_kvt.shape), _const_spec((1, d_gate)),
                _const_spec((CONV_W, 1, d_conv)), vec, vec, vec]
    args = [x, w_in, w_kvt, b_gate.reshape(1, d_gate), conv_dw.reshape(CONV_W, 1, d_conv),
            conv_b.reshape(1, d_conv), ln_g.reshape(1, d_conv), ln_b.reshape(1, d_conv)]
    aliases = {}
    if kv_all is not None:
        aliases = {len(args): 2, len(args) + 1: 3}
        in_specs += [pl.BlockSpec(memory_space=pl.ANY)] * 2
        args += list(kv_all)
    return pl.pallas_call(
        kern,
        grid=(m // tm,),
        in_specs=in_specs,
        out_specs=[row(d_pool), row(d_att)] + kv_specs + [row(d_conv), row(d_gate)] + extra_specs,
        out_shape=[jax.ShapeDtypeStruct((m, d_pool), F32), jax.ShapeDtypeStruct((m, d_att), BF16)] + kv_shapes
                  + [jax.ShapeDtypeStruct((m, d_conv), F32), jax.ShapeDtypeStruct((m, d_gate), BF16)] + extra_shapes,
        scratch_shapes=scratch,
        input_output_aliases=aliases,
        compiler_params=_cparams(("arbitrary",)),
        name="proj",
    )(*args)


def _pool_windows(load_shifted, ua, pos, c_pool):
    gw = c_pool // len(POOL_WINDOWS)
    col = lax.broadcasted_iota(jnp.int32, (1, c_pool), 1)
    run = ua
    win = None
    width = jnp.zeros((1, c_pool), jnp.int32)
    for d in range(1, max(POOL_WINDOWS)):
        run = run + load_shifted(d)
        if d + 1 in POOL_WINDOWS:
            g = POOL_WINDOWS.index(d + 1)
            in_group = (col >= g * gw) & (col < (g + 1) * gw)
            win = run if win is None else jnp.where(in_group, run, win)
            width = jnp.where(in_group, d + 1, width)
    count = jnp.minimum(pos + 1, width).astype(F32)
    return win / count - ua


def _conv_act(load_shifted, dw_ref, cb_ref, lg_ref, lb_ref):
    acc = None
    for i in range(CONV_W):
        term = load_shifted(CONV_BUF - i) * dw_ref[i]
        acc = term if acc is None else acc + term
    cc = _layernorm(acc + cb_ref[...], lg_ref[...], lb_ref[...])
    return cc * jax.nn.sigmoid(cc)


SUBLANES = 8
BRANCH_ROWS = 64


def _phase_rows(hist, ts):
    return ts + SUBLANES * ((hist - 1) // SUBLANES)


def _shifted_loader(ext_ref, phase_ref, hist, ts):
    rows = _phase_rows(hist, ts)
    for s in range(1, SUBLANES):
        phase_ref[s - 1] = ext_ref[pl.ds(s, rows), :]

    def load(d, row0, n_rows):
        m, s = divmod(hist - d, SUBLANES)
        src = ext_ref if s == 0 else phase_ref.at[s - 1]
        return src[pl.ds(SUBLANES * m + row0, n_rows), :]

    return load


def _branch_sample_kernel(pext_ref, cext_ref, dw_ref, cb_ref, lg_ref, lb_ref, p_ref, act_ref, *, start_pos, t_new):
    c_pool = pext_ref.shape[-1]
    shape = pext_ref.shape[1:]
    for t in range(t_new):
        pos = jnp.full(shape, start_pos + t, jnp.int32)
        p = _pool_windows(lambda d: pext_ref[POOL_BUF + t - d], pext_ref[POOL_BUF + t], pos, c_pool)
        p_ref[t] = p.astype(BF16)
        act = _conv_act(lambda d: cext_ref[CONV_BUF + t - d], dw_ref, cb_ref, lg_ref, lb_ref)
        act_ref[t] = act.astype(BF16)


def _branch_sample(pext, cext, conv_dw, conv_b, ln_g, ln_b, start_pos, t_new):
    _, db, c = pext.shape
    return pl.pallas_call(
        functools.partial(_branch_sample_kernel, start_pos=start_pos, t_new=t_new),
        out_shape=[jax.ShapeDtypeStruct((t_new, db, c), BF16), jax.ShapeDtypeStruct((t_new, db, c), BF16)],
        name="branch_sample",
    )(pext, cext, conv_dw.reshape(CONV_W, 1, c), conv_b.reshape(1, c), ln_g.reshape(1, c), ln_b.reshape(1, c))


def _suffix_matrix(tk):
    j = lax.broadcasted_iota(jnp.int32, (tk, tk), 0)
    s = lax.broadcasted_iota(jnp.int32, (tk, tk), 1)
    return jnp.where(j > s, 1.0, 0.0).astype(BF16)


def _sb_block(z, mask, u, carry):
    tk = z.shape[1]
    log_rest = jnp.minimum(-z, 0.0) - jnp.log(1.0 + jnp.exp(-jnp.abs(z)))
    if mask is not None:
        log_rest = jnp.where(mask, log_rest, 0.0)
    sums = _dot(log_rest.astype(BF16), u)
    right = jnp.concatenate([carry] * (tk // LANES), axis=1)
    a = jnp.exp(z + log_rest + sums + right)
    if mask is not None:
        a = jnp.where(mask, a, 0.0)
    return a, carry + jnp.sum(log_rest, axis=1, keepdims=True)


NEG_BIG = -1e30


def _attn_prompt_kernel(bias_ref, q_ref, k_ref, v_ref, u_ref, o_ref,
                        acc_ref, carry_ref, logsig_ref, rest_ref, total_ref, *, tq, tk):
    hp = pl.program_id(1)
    qi = pl.program_id(2)
    lane = lax.broadcasted_iota(jnp.int32, (tq, HEAD_PAIR), 1)
    first = lane < HEAD_DIM
    q = q_ref[...]
    zero = jnp.zeros_like(q)
    q_heads = (jnp.where(first, q, zero), jnp.where(first, zero, q))
    bias = (bias_ref[2 * hp] * LOG2_E, bias_ref[2 * hp + 1] * LOG2_E)
    acc_ref[...] = jnp.zeros_like(acc_ref)
    carry_ref[...] = jnp.zeros_like(carry_ref)

    def logits(j, slot, mask, row0=0):
        kb = k_ref[j]
        n_rows = tq - row0
        z_both = _dot(jnp.concatenate([q_heads[0][row0:], q_heads[1][row0:]], axis=0), kb)
        for h in range(2):
            z = z_both[h * n_rows:(h + 1) * n_rows] + bias[h]
            neg_abs = pltpu.bitcast(pltpu.bitcast(z, jnp.uint32) | jnp.uint32(0x80000000), F32)
            softplus = jnp.maximum(z, 0.0) + jnp.log2(1.0 + jnp.exp2(neg_abs))
            log_sig = z - softplus
            if mask is not None:
                softplus = jnp.where(mask[row0:], softplus, 0.0)
                log_sig = jnp.where(mask[row0:], log_sig, NEG_BIG)
            logsig_ref[slot, h, row0:] = log_sig
            rest_ref[slot, h, row0:] = softplus.astype(BF16)
            total_ref[slot, h, row0:] = jnp.broadcast_to(jnp.sum(softplus, axis=1, keepdims=True),
                                                         (tq - row0, LANES))

    def weigh(j, slot, row0=0):
        vb = v_ref[j]
        n_rows = tq - row0
        sums_both = _dot(jnp.concatenate([rest_ref[slot, 0, row0:], rest_ref[slot, 1, row0:]], axis=0), u_ref[...])
        weights = []
        for h in range(2):
            carry = carry_ref[h, row0:]
            right = jnp.concatenate([carry] * (tk // LANES), axis=1)
            a = jnp.exp2(logsig_ref[slot, h, row0:] + sums_both[h * n_rows:(h + 1) * n_rows] + right)
            weights.append(a.astype(BF16))
            carry_ref[h, row0:] = carry - total_ref[slot, h, row0:]
        out_both = lax.dot_general(jnp.concatenate(weights, axis=0), vb, (((1,), (1,)), ((), ())),
                                   preferred_element_type=F32)
        for h in range(2):
            acc_ref[h, row0:] += out_both[h * n_rows:(h + 1) * n_rows]

    ratio = tq // tk
    n = (qi + 1) * ratio
    row = lax.broadcasted_iota(jnp.int32, (tq, tk), 0)
    colk = lax.broadcasted_iota(jnp.int32, (tq, tk), 1)
    diag_mask = lambda i: colk + (ratio - 1 - i) * tk < row
    diag_row0 = lambda i: (ratio - 1 - i) * tk
    logits(n - 1, 0, diag_mask(0), diag_row0(0))
    for jj in range(ratio - 1):
        weigh(n - 1 - jj, jj % 2, diag_row0(jj))
        logits(n - 2 - jj, (jj + 1) % 2, diag_mask(jj + 1), diag_row0(jj + 1))

    assert ratio % 2 == 0
    first_slot = (ratio - 1) % 2

    def pair(i):
        jj = ratio - 1 + 2 * i
        weigh(n - 1 - jj, first_slot)
        logits(n - 2 - jj, 1 - first_slot, None)
        weigh(n - 2 - jj, 1 - first_slot)
        logits(n - 3 - jj, first_slot, None)

    def two_pairs(i, c):
        pair(2 * i)
        pair(2 * i + 1)
        return c

    pairs = qi * (ratio // 2)
    lax.fori_loop(0, pairs // 2, two_pairs, 0)

    @pl.when(pairs % 2 == 1)
    def _():
        pair(pairs - 1)

    weigh(0, first_slot)
    o_ref[...] = jnp.where(first, acc_ref[0], acc_ref[1]).astype(BF16)


def _attn_prompt(q, kb, vb, sb_bias, batch, seq, tq, tk):
    m, w = q.shape
    assert seq % tq == 0 and tq % tk == 0 and w % HEAD_PAIR == 0 and kb.shape == (batch, seq // tk, w, tk)
    nq = seq // tq
    qspec = pl.BlockSpec((tq, HEAD_PAIR), lambda b, h, i: (b * nq + i, h))
    kvspec = pl.BlockSpec((None, seq // tk, HEAD_PAIR, tk), lambda b, h, i: (b, 0, h, 0))
    return pl.pallas_call(
        functools.partial(_attn_prompt_kernel, tq=tq, tk=tk),
        grid=(batch, w // HEAD_PAIR, nq),
        in_specs=[pl.BlockSpec(memory_space=pltpu.SMEM), qspec, kvspec, kvspec, _const_spec((tk, tk))],
        out_specs=qspec,
        out_shape=jax.ShapeDtypeStruct((m, w), BF16),
        scratch_shapes=[pltpu.VMEM((2, tq, HEAD_PAIR), F32), pltpu.VMEM((2, tq, LANES), F32),
                        pltpu.VMEM((2, 2, tq, tk), F32), pltpu.VMEM((2, 2, tq, tk), BF16),
                        pltpu.VMEM((2, 2, tq, LANES), F32)],
        compiler_params=_cparams(("parallel", "parallel", "arbitrary")),
        name="attn_prompt",
    )(sb_bias, q, kb, vb, -_suffix_matrix(tk))


def _attn_sample_kernel(pt_ref, bias_ref, q_ref, kn_ref, vn_ref, u_ref, *rest, n_heads, t_new, pages_per_step):
    k_refs = rest[:pages_per_step]
    v_refs = rest[pages_per_step:2 * pages_per_step]
    o_ref, qrow_ref, acc_ref, carry_ref = rest[2 * pages_per_step:]
    step = pl.program_id(1)
    rows = t_new * n_heads
    width = n_heads * HEAD_DIM
    page = kn_ref.shape[1]
    own = (lax.broadcasted_iota(jnp.int32, (rows, width), 1) // HEAD_DIM
           == lax.broadcasted_iota(jnp.int32, (rows, width), 0) % n_heads)

    def block(kt, vt, mask):
        z = _dot(qrow_ref[...], kt.astype(BF16)) + bias_ref[...]
        a, carry = _sb_block(z, mask, u_ref[...], carry_ref[...])
        carry_ref[...] = carry
        acc_ref[...] += lax.dot_general(a.astype(BF16), vt.astype(BF16), (((1,), (1,)), ((), ())),
                                        preferred_element_type=F32)

    @pl.when(step == 0)
    def _():
        q = q_ref[...]
        qrep = jnp.broadcast_to(q[:, None, :], (t_new, n_heads, width)).reshape(rows, width)
        qrow_ref[...] = jnp.where(own, qrep, jnp.zeros_like(qrep))
        acc_ref[...] = jnp.zeros_like(acc_ref)
        carry_ref[...] = jnp.zeros_like(carry_ref)
        t_of_row = lax.broadcasted_iota(jnp.int32, (rows, page), 0) // n_heads
        key = lax.broadcasted_iota(jnp.int32, (rows, page), 1)
        block(kn_ref[...], vn_ref[...], key < t_of_row)

    n = pages_per_step
    stack = lambda parts: jnp.concatenate(parts, axis=0)
    kt = jnp.concatenate([r[...].astype(BF16) for r in k_refs], axis=1)
    vt = jnp.concatenate([r[...].astype(BF16) for r in v_refs], axis=1)
    z_wide = _dot(qrow_ref[...], kt)
    z = stack([z_wide[:, i * page:(i + 1) * page] for i in range(n)]) + stack([bias_ref[...]] * n)
    log_rest = jnp.minimum(-z, 0.0) - jnp.log(1.0 + jnp.exp(-jnp.abs(z)))
    sums = _dot(log_rest.astype(BF16), u_ref[...])
    totals = jnp.sum(log_rest, axis=1, keepdims=True)
    carry = carry_ref[...]
    rights = []
    for i in range(n):
        rights.append(carry)
        carry = carry + totals[i * rows:(i + 1) * rows]
    carry_ref[...] = carry
    right = jnp.concatenate([stack(rights)] * (page // LANES), axis=1)
    a = jnp.exp(z + log_rest + sums + right).astype(BF16)
    a_wide = jnp.concatenate([a[i * rows:(i + 1) * rows] for i in range(n)], axis=1)
    acc_ref[...] += lax.dot_general(a_wide, vt, (((1,), (1,)), ((), ())), preferred_element_type=F32)

    @pl.when(step == pl.num_programs(1) - 1)
    def _():
        picked = jnp.where(own, acc_ref[...], 0.0).reshape(t_new, n_heads, width)
        o_ref[...] = jnp.sum(picked, axis=1).astype(BF16)


def _attn_sample(q, k_new, v_new, cache_k, cache_v, layer, page_table, sb_bias, pages_per_step):
    db, t_new, w = q.shape
    n_heads = w // HEAD_DIM
    depth, pool_pages, page = cache_k.shape[:3]
    n_pages = page_table.shape[1]
    assert n_pages % pages_per_step == 0 and page % LANES == 0 and t_new <= page
    rows = t_new * n_heads
    paged = lambda c: c.transpose(0, 1, 3, 4, 2).reshape(depth, pool_pages, w, page)
    new_page = lambda a: jnp.pad(a.transpose(0, 2, 1), ((0, 0), (0, 0), (0, page - t_new)))
    bias_rows = jnp.tile(sb_bias, t_new).reshape(rows, 1).astype(F32)

    def page_spec(i):
        def idx(b, s, pt):
            return (layer, pt[b * n_pages + (n_pages - 1 - (s * pages_per_step + i))], 0, 0)
        return pl.BlockSpec((None, None, w, page), idx)

    per_b = lambda r, c: pl.BlockSpec((None, r, c), lambda b, s, pt: (b, 0, 0))
    const2 = lambda shape: pl.BlockSpec(shape, lambda b, s, pt: (0, 0))
    grid_spec = pltpu.PrefetchScalarGridSpec(
        num_scalar_prefetch=1,
        grid=(db, n_pages // pages_per_step),
        in_specs=[const2((rows, 1)), per_b(t_new, w), per_b(w, page), per_b(w, page), const2((page, page))]
                 + [page_spec(i) for i in range(pages_per_step)] * 2,
        out_specs=per_b(t_new, w),
        scratch_shapes=[pltpu.VMEM((rows, w), BF16), pltpu.VMEM((rows, w), F32), pltpu.VMEM((rows, LANES), F32)],
    )
    return pl.pallas_call(
        functools.partial(_attn_sample_kernel, n_heads=n_heads, t_new=t_new, pages_per_step=pages_per_step),
        grid_spec=grid_spec,
        out_shape=jax.ShapeDtypeStruct((db, t_new, w), BF16),
        compiler_params=_cparams(("parallel", "arbitrary")),
        name="attn_sample",
    )(page_table.reshape(-1), bias_rows, q, new_page(k_new), new_page(v_new), _suffix_matrix(page),
      *([paged(cache_k)] * pages_per_step), *([paged(cache_v)] * pages_per_step))


MERGE_ROWS = 512

def _merge_kernel(x_ref, p_ref, o_ref, act_ref, gate_ref, wp_ref, ps_ref, wo_ref, wc_ref, wout_ref, lg_ref, lb_ref,
                  y_ref, *, alpha):
    tm, d = x_ref.shape
    sub = min(tm, MERGE_ROWS)
    w_o, w_c, w_out = wo_ref[...].astype(BF16), wc_ref[...].astype(BF16), wout_ref[...].astype(BF16)
    for r in range(0, tm, sub):
        rows = slice(r, r + sub)
        br_a = _dot(p_ref[rows, :], wp_ref[...]) * ps_ref[...]
        br_b = _dot(o_ref[rows, :], w_o)
        br_c = _dot(act_ref[rows, :], w_c)
        merged = (gate_ref[rows, 0:d].astype(F32) * br_a + gate_ref[rows, d:2 * d].astype(F32) * br_b
                  + gate_ref[rows, 2 * d:3 * d].astype(F32) * br_c)
        y = _dot(merged.astype(BF16), w_out)
        y_ref[rows, :] = _layernorm(alpha * x_ref[rows, :] + y, lg_ref[...], lb_ref[...])


def _merge(x, p, o, act, gates, w_pool, pool_scale, w_att_o, conv_pw, w_out, ln_g, ln_b, alpha, tm, layer):
    m, d = x.shape
    assert m % tm == 0
    row = lambda a: pl.BlockSpec((tm, a.shape[1]), lambda i: (i, 0))
    full = lambda a: _layer_spec(a, layer) if a.ndim == 3 else _const_spec(a.shape)
    vec = lambda a: a.reshape(1, d)
    args = (x, p, o, act, gates, w_pool, vec(pool_scale), w_att_o, conv_pw, w_out, vec(ln_g), vec(ln_b))
    return pl.pallas_call(
        functools.partial(_merge_kernel, alpha=alpha),
        grid=(m // tm,),
        in_specs=[row(a) for a in args[:5]] + [full(a) for a in args[5:]],
        out_specs=pl.BlockSpec((tm, d), lambda i: (i, 0)),
        out_shape=jax.ShapeDtypeStruct((m, d), F32),
        compiler_params=_cparams(("parallel",)),
        name="merge",
    )(*args)


def _ffn_kernel(x_ref, wg_ref, wu_ref, wd_ref, lg_ref, lb_ref, y_ref, *, alpha, chunk):
    x = x_ref[...]
    xb = x.astype(BF16)
    d_ff = wg_ref.shape[1]
    acc = None
    for c in range(0, d_ff, chunk):
        n = min(chunk, d_ff - c)
        g = _dot(xb, wg_ref[:, c:c + n].astype(BF16))
        u = _dot(xb, wu_ref[:, c:c + n].astype(BF16))
        h = (g * jax.nn.sigmoid(g) * u).astype(BF16)
        part = _dot(h, wd_ref[c:c + n, :].astype(BF16))
        acc = part if acc is None else acc + part
    y_ref[...] = _layernorm(alpha * x + acc, lg_ref[...], lb_ref[...])


def _ffn(x, wg, wu, wd, ln_g, ln_b, alpha, tm, layer):
    m, d = x.shape
    assert m % tm == 0
    vec = lambda a: a.reshape(1, d)
    return pl.pallas_call(
        functools.partial(_ffn_kernel, alpha=alpha, chunk=512),
        grid=(m // tm,),
        in_specs=[pl.BlockSpec((tm, d), lambda i: (i, 0)), _layer_spec(wg, layer), _layer_spec(wu, layer),
                  _layer_spec(wd, layer), _const_spec((1, d)), _const_spec((1, d))],
        out_specs=pl.BlockSpec((tm, d), lambda i: (i, 0)),
        out_shape=jax.ShapeDtypeStruct((m, d), F32),
        compiler_params=_cparams(("parallel",)),
        name="ffn",
    )(x, wg, wu, wd, vec(ln_g), vec(ln_b))


def _row_tile(m, want):
    t = min(want, m)
    while m % t:
        t //= 2
    return t


def kernel(x_prompt, x_sample, cache_k, cache_v, state_pool, state_conv, page_table, w_in, b_gate, sb_bias, pool_w, pool_scale, w_att_o, conv_dw, conv_b, conv_ln_g, conv_ln_b, conv_pw, w_out, ln1_g, ln1_b, ffn_w_gate, ffn_w_up, ffn_w_down, ln2_g, ln2_b):
    depth = w_in.shape[0]
    batch, seq, d = x_prompt.shape
    db, t_new, _ = x_sample.shape
    n_pages = page_table.shape[1]
    page = cache_k.shape[2]
    past = n_pages * page
    alpha = (2.0 * depth) ** 0.25
    w_att = d // 2
    n_heads = w_att // HEAD_DIM
    c_pool = d // 4
    n_groups = len(POOL_WINDOWS)
    gw = c_pool // n_groups

    xp = x_prompt.reshape(batch * seq, d)
    xs = x_sample.reshape(db * t_new, d)
    mp, ms = xp.shape[0], xs.shape[0]
    tm_p = _row_tile(mp, 512)
    tm_s = _row_tile(ms, 128)
    tk = _row_tile(seq, 256)
    tq = _row_tile(seq, 512)

    outs = {n: [] for n in ("pp", "cp", "ks", "vs", "ps", "cs")}
    kv_all = None
    for l in range(depth):
        w_kvt = lax.optimization_barrier(w_in[l, :, c_pool + w_att:c_pool + 3 * w_att]).T.astype(BF16)
        w_pool = jnp.zeros((c_pool, d), F32)
        for g in range(n_groups):
            w_pool = w_pool.at[g * gw:(g + 1) * gw, g * (d // n_groups):(g + 1) * (d // n_groups)].set(pool_w[l, g])
        w_pool = w_pool.astype(BF16)

        def tail(x, p, o, act, gates, tm):
            x1 = _merge(x, p, o, act, gates, w_pool, pool_scale[l], w_att_o, conv_pw, w_out, ln1_g[l], ln1_b[l], alpha,
                        _row_tile(x.shape[0], 2 * tm), l)
            return _ffn(x1, ffn_w_gate, ffn_w_up, ffn_w_down, ln2_g[l], ln2_b[l], alpha, tm, l)

        conv = (conv_dw[l], conv_b[l], conv_ln_g[l], conv_ln_b[l])
        ua, q, kt, vt, ktb, vtb, glu, gates, p, act = _proj(
            xp, w_in, w_kvt, b_gate[l], conv, tm_p, HEAD_DIM ** -0.5 * LOG2_E,
            seq=seq, kv_block=tk, layer=l, depth=depth, kv_all=kv_all)
        kv_all = (kt, vt)
        o = _attn_prompt(q, ktb, vtb, sb_bias[l], batch, seq, tq, tk)
        xp = tail(xp, p, o, act, gates, tm_p)
        outs["pp"].append(ua.reshape(batch, seq, c_pool)[:, seq - POOL_BUF:])
        outs["cp"].append(glu.reshape(batch, seq, c_pool)[:, seq - CONV_BUF:])

        ua, q, k, v, kb, vb, glu, gates = _proj(xs, w_in, w_kvt, b_gate[l], conv, tm_s, HEAD_DIM ** -0.5, layer=l)
        pext = jnp.concatenate([state_pool[l], ua.reshape(db, t_new, c_pool)], axis=1)
        cext = jnp.concatenate([state_conv[l], glu.reshape(db, t_new, c_pool)], axis=1)
        p, act = _branch_sample(pext.transpose(1, 0, 2), cext.transpose(1, 0, 2), conv_dw[l], conv_b[l],
                                conv_ln_g[l], conv_ln_b[l], past, t_new)
        p = p.transpose(1, 0, 2).reshape(ms, c_pool)
        act = act.transpose(1, 0, 2).reshape(ms, c_pool)
        per_b = lambda a: a.reshape(db, t_new, w_att)
        heads = lambda a: a.reshape(db, t_new, n_heads, HEAD_DIM)
        o = _attn_sample(per_b(q), per_b(kb), per_b(vb), cache_k, cache_v, l, page_table, sb_bias[l],
                         pages_per_step=_row_tile(n_pages, 32))
        xs = tail(xs, p, o.reshape(ms, w_att), act, gates, tm_s)
        outs["ks"].append(heads(k))
        outs["vs"].append(heads(v))
        outs["ps"].append(pext[:, -POOL_BUF:])
        outs["cs"].append(cext[:, -CONV_BUF:])

    st = lambda n: jnp.stack(outs[n])
    per_head = lambda a: a.reshape(depth, batch, n_heads, HEAD_DIM, seq).transpose(0, 1, 4, 2, 3)
    return (xp.reshape(batch, seq, d), xs.reshape(db, t_new, d), per_head(kv_all[0]), per_head(kv_all[1]),
            st("pp"), st("cp"),
            st("ks"), st("vs"), st("ps"), st("cs"))
```
